```python
import jax, jax.numpy as jnp
from jax import lax
import numpy as np

D_MODEL = 2048
BATCH = 4
SEQ = 2048
DEPTH = 1
DEC_BATCH = 16
DEC_SEQ = 2048
PAST_LEN = 128

HEAD_DIM = 128
N_Q_HEADS = D_MODEL // HEAD_DIM
N_KV_HEADS = N_Q_HEADS // 4
Q_PER_KV = N_Q_HEADS // N_KV_HEADS
ATTN_WIDTH = N_Q_HEADS * HEAD_DIM
KV_WIDTH = N_KV_HEADS * HEAD_DIM
N_FOURIER_GROUPS = 4
FOURIER_WIDTH = D_MODEL // 2
FOURIER_GROUP_DIM = FOURIER_WIDTH // N_FOURIER_GROUPS
IN_WIDTH = ATTN_WIDTH + 2 * KV_WIDTH + FOURIER_WIDTH
WINDOW = 128
BLOCK = 128
D_FF = -(-8 * D_MODEL // (3 * 256)) * 256
N_MOD = 6
RMS_EPS = 1e-6

kernel_name = "hybrid_gated_fnet_swa_encoder"


def rmsnorm(x, g):
    xf = x.astype(jnp.float32)
    y = xf * lax.rsqrt(jnp.mean(xf * xf, axis=-1, keepdims=True) + RMS_EPS)
    return (y * g.astype(jnp.float32)).astype(x.dtype)


def alibi_slopes():
    h = jnp.arange(1, N_Q_HEADS + 1, dtype=jnp.float32)
    return jnp.exp2(-8.0 * h / N_Q_HEADS).reshape(N_KV_HEADS, Q_PER_KV)


def banded_window_attention(q, k, v, sink):
    B, S = q.shape[0], q.shape[1]
    nb = S // BLOCK
    pad = ((0, 0), (BLOCK, BLOCK), (0, 0), (0, 0))
    kp = jnp.pad(k, pad)
    vp = jnp.pad(v, pad)
    qb = jnp.moveaxis(q.reshape(B, nb, BLOCK, N_KV_HEADS, Q_PER_KV, HEAD_DIM), 1, 0)
    a = jnp.arange(BLOCK)[:, None]
    j = jnp.arange(3 * BLOCK)[None, :]
    rel = j - BLOCK - a
    dist = jnp.abs(rel).astype(jnp.float32)
    in_window = jnp.abs(rel) <= WINDOW
    slopes = alibi_slopes()
    bias = -slopes[:, :, None, None] * dist[None, None]
    sink_l = sink.astype(jnp.float32).reshape(N_KV_HEADS, Q_PER_KV)
    scale = HEAD_DIM ** -0.5

    def one_block(args):
        i, qi = args
        start = i * BLOCK
        ki = lax.dynamic_slice_in_dim(kp, start, 3 * BLOCK, axis=1)
        vi = lax.dynamic_slice_in_dim(vp, start, 3 * BLOCK, axis=1)
        key_pos = start - BLOCK + jnp.arange(3 * BLOCK)
        valid = in_window & ((key_pos >= 0) & (key_pos < S))[None, :]
        s = jnp.einsum('bqkgd,bskd->bkgqs', qi, ki,
                       preferred_element_type=jnp.float32) * scale + bias
        s = jnp.where(valid, s, -jnp.inf)
        sink_col = jnp.broadcast_to(sink_l[None, :, :, None, None], s.shape[:-1] + (1,))
        p = jax.nn.softmax(jnp.concatenate([s, sink_col], axis=-1), axis=-1)[..., :-1]
        return jnp.einsum('bkgqs,bskd->bqkgd', p.astype(vi.dtype), vi)

    out = lax.map(one_block, (jnp.arange(nb), qb))
    return jnp.moveaxis(out, 0, 1).reshape(B, S, ATTN_WIDTH)


def fourier_mix(u):
    B, S = u.shape[0], u.shape[1]
    ug = u.astype(jnp.float32).reshape(B, S, N_FOURIER_GROUPS, FOURIER_GROUP_DIM)
    z = jnp.fft.fft2(ug, axes=(1, 3), norm="ortho")
    return jnp.real(z).reshape(B, S, FOURIER_WIDTH).astype(u.dtype)


def encoder_layer(x, c, w_mod, b_mod, g_mix, w_in, attn_sink, w_attn_branch,
                  w_fourier_branch, w_gate, b_gate, w_out, g_ffn, w_up, w_down):
    B, S = x.shape[0], x.shape[1]
    mod = (jax.nn.silu(c) @ w_mod + b_mod)[:, None, :]
    shift1, scale1, gate1, shift2, scale2, gate2 = jnp.split(mod, N_MOD, axis=-1)

    h = rmsnorm(x, g_mix) * (1.0 + scale1) + shift1
    proj = h @ w_in
    q, k, v, u = jnp.split(proj, [ATTN_WIDTH, ATTN_WIDTH + KV_WIDTH,
                                  ATTN_WIDTH + 2 * KV_WIDTH], axis=-1)
    q = q.reshape(B, S, N_KV_HEADS, Q_PER_KV, HEAD_DIM)
    k = k.reshape(B, S, N_KV_HEADS, HEAD_DIM)
    v = v.reshape(B, S, N_KV_HEADS, HEAD_DIM)
    attn = banded_window_attention(q, k, v, attn_sink) @ w_attn_branch
    four = fourier_mix(u) @ w_fourier_branch
    g_attn, g_four = jnp.split(jax.nn.sigmoid(h @ w_gate + b_gate), 2, axis=-1)
    merged = g_attn * attn + g_four * four
    x = x + gate1 * (merged @ w_out)

    h2 = rmsnorm(x, g_ffn) * (1.0 + scale2) + shift2
    gt, up = jnp.split(h2 @ w_up, 2, axis=-1)
    x = x + gate2 * ((jax.nn.silu(gt) * up) @ w_down)
    return x


def trunk(x, c, w_mod, b_mod, g_mix, w_in, attn_sink, w_attn_branch, w_fourier_branch,
          w_gate, b_gate, w_out, g_ffn, w_up, w_down, g_final):
    for l in range(DEPTH):
        x = encoder_layer(x, c, w_mod[l], b_mod[l], g_mix[l], w_in[l], attn_sink[l],
                          w_attn_branch[l], w_fourier_branch[l], w_gate[l], b_gate[l],
                          w_out[l], g_ffn[l], w_up[l], w_down[l])
    return rmsnorm(x, g_final)


def setup_inputs(seed: int = 0) -> dict:
    key = jax.random.key(seed)
    ks = jax.random.split(key, 20)
    f32 = jnp.float32

    def dense(k, fan_in, fan_out, gain=1.0):
        return jax.random.normal(k, (DEPTH, fan_in, fan_out), f32) * (gain * fan_in ** -0.5)

    def gain_vec(k, n):
        return 1.0 + 0.02 * jax.random.normal(k, (DEPTH, n), f32)

    return {
        "x_prompt": jax.random.normal(ks[0], (BATCH, SEQ, D_MODEL), f32),
        "x_sample": jax.random.normal(ks[1], (DEC_BATCH, DEC_SEQ, D_MODEL), f32),
        "c_prompt": jax.random.normal(ks[2], (BATCH, D_MODEL), f32),
        "c_sample": jax.random.normal(ks[3], (DEC_BATCH, D_MODEL), f32),
        "w_mod": dense(ks[4], D_MODEL, N_MOD * D_MODEL, 0.5),
        "b_mod": 0.02 * jax.random.normal(ks[5], (DEPTH, N_MOD * D_MODEL), f32),
        "g_mix": gain_vec(ks[6], D_MODEL),
        "w_in": dense(ks[7], D_MODEL, IN_WIDTH),
        "attn_sink": 0.5 * jax.random.normal(ks[8], (DEPTH, N_Q_HEADS), f32),
        "w_attn_branch": dense(ks[9], ATTN_WIDTH, D_MODEL),
        "w_fourier_branch": dense(ks[10], FOURIER_WIDTH, D_MODEL),
        "w_gate": dense(ks[11], D_MODEL, 2 * D_MODEL),
        "b_gate": 0.02 * jax.random.normal(ks[12], (DEPTH, 2 * D_MODEL), f32),
        "w_out": dense(ks[13], D_MODEL, D_MODEL),
        "g_ffn": gain_vec(ks[14], D_MODEL),
        "w_up": dense(ks[15], D_MODEL, 2 * D_FF),
        "w_down": dense(ks[16], D_FF, D_MODEL),
        "g_final": 1.0 + 0.02 * jax.random.normal(ks[17], (D_MODEL,), f32),
    }


def reference(x_prompt, x_sample, c_prompt, c_sample, w_mod, b_mod, g_mix, w_in,
              attn_sink, w_attn_branch, w_fourier_branch, w_gate, b_gate, w_out,
              g_ffn, w_up, w_down, g_final):
    y_prompt = trunk(x_prompt, c_prompt, w_mod, b_mod, g_mix, w_in, attn_sink,
                     w_attn_branch, w_fourier_branch, w_gate, b_gate, w_out,
                     g_ffn, w_up, w_down, g_final)
    y_sample = trunk(x_sample, c_sample, w_mod, b_mod, g_mix, w_in, attn_sink,
                     w_attn_branch, w_fourier_branch, w_gate, b_gate, w_out,
                     g_ffn, w_up, w_down, g_final)
    return (y_prompt, y_sample)
```

```python
import functools

import jax
import jax.numpy as jnp
from jax import lax
from jax.experimental import pallas as pl
from jax.experimental.pallas import tpu as pltpu

F32 = jnp.float32
BF16 = jnp.bfloat16

HEAD_DIM = 128
Q_PER_KV = 4
N_FOURIER_GROUPS = 4
WINDOW = 128
BLOCK = 128
N_MOD = 6
RMS_EPS = 1e-6

V7X_VMEM_LIMIT_BYTES = 60 * 1024 * 1024
BF16_SUBLANE_TILE = 16


def _params(*semantics):
    return pltpu.CompilerParams(dimension_semantics=semantics,
                                vmem_limit_bytes=V7X_VMEM_LIMIT_BYTES)


def _resident(block_shape, index_map):
    return pl.BlockSpec(block_shape, index_map, pipeline_mode=pl.Buffered(1))


def _rms_modulate(x, gain, scale, shift):
    y = x * lax.rsqrt(jnp.mean(x * x, axis=-1, keepdims=True) + RMS_EPS)
    return (y * gain) * (1.0 + scale) + shift


def _mod_kernel(c_ref, w_ref, b_ref, o_ref):
    c = c_ref[...]
    a = (c * jax.nn.sigmoid(c)).astype(BF16)
    o_ref[...] = jnp.dot(a, w_ref[...].astype(BF16), preferred_element_type=F32) + b_ref[...]


def _modulation(c, w_mod, b_mod, *, tn=512):
    bp, d = c.shape
    n = w_mod.shape[1]
    return pl.pallas_call(
        _mod_kernel,
        out_shape=jax.ShapeDtypeStruct((bp, n), F32),
        grid=(n // tn,),
        in_specs=[pl.BlockSpec((bp, d), lambda j: (0, 0)),
                  pl.BlockSpec((d, tn), lambda j: (0, j)),
                  pl.BlockSpec((1, tn), lambda j: (0, j))],
        out_specs=pl.BlockSpec((bp, tn), lambda j: (0, j)),
        name="modulation",
        compiler_params=_params("arbitrary"),
    )(c, w_mod, b_mod.reshape(1, n))


def _proj_kernel(x_ref, mod_ref, g_ref, w_ref, b_ref, o_ref, h_ref, *, n_in_tiles, row_chunk):
    j = pl.program_id(1)

    @pl.when(j == 0)
    def _():
        shift = mod_ref[0, 0:1, :]
        scale = mod_ref[0, 1:2, :]
        gain = g_ref[...]

        def body(r, carry):
            rows = pl.ds(pl.multiple_of(r * row_chunk, row_chunk), row_chunk)
            h_ref[rows, :] = _rms_modulate(x_ref[rows, :], gain, scale, shift).astype(BF16)
            return carry

        lax.fori_loop(0, x_ref.shape[0] // row_chunk, body, 0)

    @pl.when(j < n_in_tiles)
    def _():
        o_ref[...] = jnp.dot(h_ref[...], w_ref[...], preferred_element_type=F32).astype(BF16)

    @pl.when(j >= n_in_tiles)
    def _():
        acc = jnp.dot(h_ref[...], w_ref[...], preferred_element_type=F32)
        o_ref[...] = jax.nn.sigmoid(acc + b_ref[...]).astype(BF16)


def _projection(x2d, mod, b_off, seq, g_mix, w_cat, b_gate, *, in_width, tm=1024, tn=1024):
    m, d = x2d.shape
    n = w_cat.shape[1]
    n_in_tiles = in_width // tn
    tiles_per_seq = seq // tm
    kern = functools.partial(_proj_kernel, n_in_tiles=n_in_tiles, row_chunk=256)
    return pl.pallas_call(
        kern,
        out_shape=jax.ShapeDtypeStruct((m, n), BF16),
        grid=(m // tm, n // tn),
        in_specs=[pl.BlockSpec((tm, d), lambda i, j: (i, 0)),
                  pl.BlockSpec((1, N_MOD, d), lambda i, j: (b_off + i // tiles_per_seq, 0, 0)),
                  pl.BlockSpec((1, d), lambda i, j: (0, 0)),
                  pl.BlockSpec((d, tn), lambda i, j: (0, j)),
                  pl.BlockSpec((1, tn), lambda i, j: (0, jnp.maximum(j - n_in_tiles, 0)))],
        out_specs=pl.BlockSpec((tm, tn), lambda i, j: (i, j)),
        scratch_shapes=[pltpu.VMEM((tm, d), BF16)],
        name="projection",
        compiler_params=_params("parallel", "arbitrary"),
    )(x2d, mod, g_mix.reshape(1, d), w_cat, b_gate.reshape(1, -1))


def _attn_kernel(slopes_ref, sink_ref, q_ref, k_ref, v_ref, o_ref, *, seq):
    kh = pl.program_id(1)
    span = 3 * BLOCK
    scale = HEAD_DIM ** -0.5

    def body(i, carry):
        q0 = pl.multiple_of(i * BLOCK, BLOCK)
        k0 = pl.multiple_of(jnp.clip(q0 - BLOCK, 0, seq - span), BLOCK)
        q = q_ref[pl.ds(q0, BLOCK), :]
        kw = k_ref[pl.ds(k0, span), :]
        vw = v_ref[pl.ds(k0, span), :]
        rel = ((k0 - q0) + lax.broadcasted_iota(jnp.int32, (BLOCK, span), 1)
               - lax.broadcasted_iota(jnp.int32, (BLOCK, span), 0))
        valid = jnp.abs(rel) <= WINDOW
        dist = jnp.abs(rel).astype(F32)
        for g in range(Q_PER_KV):
            head = kh * Q_PER_KV + g
            slope = slopes_ref[head]
            sink = sink_ref[head]
            cols = slice(g * HEAD_DIM, (g + 1) * HEAD_DIM)
            s = lax.dot_general(q[:, cols], kw, (((1,), (1,)), ((), ())),
                                preferred_element_type=F32)
            s = jnp.where(valid, s * scale - slope * dist, -jnp.inf)
            mx = jnp.maximum(jnp.max(s, axis=-1, keepdims=True), sink)
            p = jnp.exp(s - mx)
            denom = jnp.sum(p, axis=-1, keepdims=True) + jnp.exp(sink - mx)
            o = jnp.dot(p.astype(BF16), vw, preferred_element_type=F32)
            o_ref[pl.ds(q0, BLOCK), cols] = (o / denom).astype(BF16)
        return carry

    lax.fori_loop(0, seq // BLOCK, body, 0)


def _attention(proj, slopes, sink, batch, seq, *, attn_width, kv_width):
    m = proj.shape[0]
    n_kv = kv_width // HEAD_DIM
    gw = Q_PER_KV * HEAD_DIM
    k_col0 = attn_width // HEAD_DIM
    v_col0 = (attn_width + kv_width) // HEAD_DIM
    smem = pl.BlockSpec(memory_space=pltpu.SMEM)
    return pl.pallas_call(
        functools.partial(_attn_kernel, seq=seq),
        out_shape=jax.ShapeDtypeStruct((m, attn_width), BF16),
        grid=(batch, n_kv),
        in_specs=[smem, smem,
                  pl.BlockSpec((seq, gw), lambda b, h: (b, h)),
                  pl.BlockSpec((seq, HEAD_DIM), lambda b, h: (b, k_col0 + h)),
                  pl.BlockSpec((seq, HEAD_DIM), lambda b, h: (b, v_col0 + h))],
        out_specs=pl.BlockSpec((seq, gw), lambda b, h: (b, h)),
        name="attention",
        compiler_params=_params("parallel", "arbitrary"),
    )(slopes, sink, proj, proj, proj)


def _fourier_kernel(u_ref, cd_ref, cs_ref, o_ref, ab_ref, *, seq, gd):
    @pl.when(pl.program_id(1) == 0)
    def _():
        for g in range(N_FOURIER_GROUPS):
            cols = slice(g * gd, (g + 1) * gd)
            t = jnp.dot(u_ref[:, cols], cd_ref[...], preferred_element_type=F32)
            ab_ref[0:seq, cols] = t[:, :gd].astype(BF16)
            ab_ref[seq:2 * seq, cols] = t[:, gd:].astype(BF16)

    o_ref[...] = jnp.dot(cs_ref[...], ab_ref[...], preferred_element_type=F32).astype(BF16)


def _dft_tables(seq, gd):
    def table(n):
        idx = jnp.arange(n, dtype=jnp.int32)
        ang = ((idx[:, None] * idx[None, :]) % n).astype(F32) * (2.0 * jnp.pi / n)
        return jnp.cos(ang), jnp.sin(ang)
    cd, sd = table(gd)
    cs, ss = table(seq)
    chan = jnp.concatenate([cd, sd], axis=1) * (gd ** -0.5)
    pos = jnp.concatenate([cs, -ss], axis=1) * (seq ** -0.5)
    return chan.astype(BF16), pos.astype(BF16)


def _fourier(proj, chan_tab, pos_tab, batch, seq, *, u_col0, f_width, tm=512):
    m = proj.shape[0]
    gd = f_width // N_FOURIER_GROUPS
    return pl.pallas_call(
        functools.partial(_fourier_kernel, seq=seq, gd=gd),
        out_shape=jax.ShapeDtypeStruct((m, f_width), BF16),
        grid=(batch, seq // tm),
        in_specs=[pl.BlockSpec((seq, f_width), lambda b, i: (b, u_col0 // f_width)),
                  _resident((gd, 2 * gd), lambda b, i: (0, 0)),
                  pl.BlockSpec((tm, 2 * seq), lambda b, i: (i, 0))],
        out_specs=pl.BlockSpec((tm, f_width), lambda b, i: (b * (seq // tm) + i, 0)),
        scratch_shapes=[pltpu.VMEM((2 * seq, f_width), BF16)],
        name="fourier",
        compiler_params=_params("parallel", "arbitrary"),
    )(proj, chan_tab, pos_tab)


def _merge_kernel(a_ref, f_ref, ga_ref, gf_ref, wa_ref, wf_ref, o_ref):
    ta = jnp.dot(a_ref[...], wa_ref[...], preferred_element_type=F32)
    tf = jnp.dot(f_ref[...], wf_ref[...], preferred_element_type=F32)
    o_ref[...] = (ga_ref[...].astype(F32) * ta + gf_ref[...].astype(F32) * tf).astype(BF16)


def _merge(attn, four, projg, wa, wf, *, gate_col0, tm=1024, tn=1024):
    m, aw = attn.shape
    fw = four.shape[1]
    d = wa.shape[1]
    ga0 = gate_col0 // tn
    gf0 = (gate_col0 + d) // tn
    return pl.pallas_call(
        _merge_kernel,
        out_shape=jax.ShapeDtypeStruct((m, d), BF16),
        grid=(m // tm, d // tn),
        in_specs=[pl.BlockSpec((tm, aw), lambda i, j: (i, 0)),
                  pl.BlockSpec((tm, fw), lambda i, j: (i, 0)),
                  pl.BlockSpec((tm, tn), lambda i, j: (i, ga0 + j)),
                  pl.BlockSpec((tm, tn), lambda i, j: (i, gf0 + j)),
                  pl.BlockSpec((aw, tn), lambda i, j: (0, j)),
                  pl.BlockSpec((fw, tn), lambda i, j: (0, j))],
        out_specs=pl.BlockSpec((tm, tn), lambda i, j: (i, j)),
        name="merge",
        compiler_params=_params("parallel", "arbitrary"),
    )(attn, four, projg, projg, wa, wf)


def _out_kernel(m_ref, x_ref, mod_ref, g_ref, wo_ref, x1_ref, h2_ref):
    out = jnp.dot(m_ref[...], wo_ref[...], preferred_element_type=F32)
    x1 = x_ref[...] + mod_ref[0, 2:3, :] * out
    x1_ref[...] = x1
    h2 = _rms_modulate(x1, g_ref[...], mod_ref[0, 4:5, :], mod_ref[0, 3:4, :])
    h2_ref[...] = h2.astype(BF16)


def _out_proj(merged, x2d, mod, b_off, seq, g_ffn, wo, *, tm=512):
    m, d = x2d.shape
    tiles_per_seq = seq // tm
    return pl.pallas_call(
        _out_kernel,
        out_shape=(jax.ShapeDtypeStruct((m, d), F32), jax.ShapeDtypeStruct((m, d), BF16)),
        grid=(m // tm,),
        in_specs=[pl.BlockSpec((tm, d), lambda i: (i, 0)),
                  pl.BlockSpec((tm, d), lambda i: (i, 0)),
                  pl.BlockSpec((1, N_MOD, d), lambda i: (b_off + i // tiles_per_seq, 0, 0)),
                  pl.BlockSpec((1, d), lambda i: (0, 0)),
                  _resident((d, d), lambda i: (0, 0))],
        out_specs=(pl.BlockSpec((tm, d), lambda i: (i, 0)),
                   pl.BlockSpec((tm, d), lambda i: (i, 0))),
        name="out_proj",
        compiler_params=_params("parallel"),
    )(merged, x2d, mod, g_ffn.reshape(1, d), wo)


def _ffn_up_kernel(h_ref, wg_ref, wu_ref, o_ref):
    h = h_ref[...]
    gt = jnp.dot(h, wg_ref[...], preferred_element_type=F32)
    up = jnp.dot(h, wu_ref[...], preferred_element_type=F32)
    o_ref[...] = (gt * jax.nn.sigmoid(gt) * up).astype(BF16)


def _ffn_up(h2, w_up, *, tm=1024, tn=1408):
    m, d = h2.shape
    dff = w_up.shape[1] // 2
    nj = dff // tn
    return pl.pallas_call(
        _ffn_up_kernel,
        out_shape=jax.ShapeDtypeStruct((m, dff), BF16),
        grid=(nj, m // tm),
        in_specs=[pl.BlockSpec((tm, d), lambda j, i: (i, 0)),
                  _resident((d, tn), lambda j, i: (0, j)),
                  _resident((d, tn), lambda j, i: (0, nj + j))],
        out_specs=pl.BlockSpec((tm, tn), lambda j, i: (i, j)),
        name="ffn_up",
        compiler_params=_params("arbitrary", "parallel"),
    )(h2, w_up, w_up)


def _ffn_down_kernel(a_ref, x1_ref, mod_ref, g_ref, wd_ref, y_ref, *, final_norm):
    out = jnp.dot(a_ref[...], wd_ref[...], preferred_element_type=F32)
    x2 = x1_ref[...] + mod_ref[0, 5:6, :] * out
    if final_norm:
        y = x2 * lax.rsqrt(jnp.mean(x2 * x2, axis=-1, keepdims=True) + RMS_EPS)
        x2 = y * g_ref[...]
    y_ref[...] = x2


def _ffn_down(act, x1, mod, b_off, seq, g_final, wd, *, final_norm, tm=256):
    m, d = x1.shape
    dff = act.shape[1]
    tiles_per_seq = seq // tm
    return pl.pallas_call(
        functools.partial(_ffn_down_kernel, final_norm=final_norm),
        out_shape=jax.ShapeDtypeStruct((m, d), F32),
        grid=(m // tm,),
        in_specs=[pl.BlockSpec((tm, dff), lambda i: (i, 0)),
                  pl.BlockSpec((tm, d), lambda i: (i, 0)),
                  pl.BlockSpec((1, N_MOD, d), lambda i: (b_off + i // tiles_per_seq, 0, 0)),
                  pl.BlockSpec((1, d), lambda i: (0, 0)),
                  _resident((dff, d), lambda i: (0, 0))],
        out_specs=pl.BlockSpec((tm, d), lambda i: (i, 0)),
        name="ffn_down",
        compiler_params=_params("parallel"),
    )(act, x1, mod, g_final.reshape(1, d), wd)


def kernel(x_prompt, x_sample, c_prompt, c_sample, w_mod, b_mod, g_mix, w_in, attn_sink,
           w_attn_branch, w_fourier_branch, w_gate, b_gate, w_out, g_ffn, w_up, w_down,
           g_final):
    depth, d, in_width = w_in.shape
    attn_width = w_attn_branch.shape[1]
    f_width = w_fourier_branch.shape[1]
    kv_width = (in_width - attn_width - f_width) // 2
    n_heads = attn_width // HEAD_DIM
    groups = [(x_prompt, 0), (x_sample, c_prompt.shape[0])]
    seq = x_prompt.shape[1]
    assert x_sample.shape[1] == seq and seq % BLOCK == 0 and seq >= 3 * BLOCK
    assert (attn_width + 2 * kv_width) % f_width == 0

    c_all = jnp.concatenate([c_prompt, c_sample], axis=0)
    pad = -c_all.shape[0] % BF16_SUBLANE_TILE
    c_all = jnp.pad(c_all, ((0, pad), (0, 0)))

    heads = jnp.arange(1, n_heads + 1, dtype=F32)
    slopes = jnp.exp2(-8.0 * heads / n_heads)
    chan_tab, pos_tab = _dft_tables(seq, f_width // N_FOURIER_GROUPS)

    xs = [x.reshape(-1, d) for x, _ in groups]
    for l in range(depth):
        last = l == depth - 1
        mod = _modulation(c_all, w_mod[l], b_mod[l]).reshape(-1, N_MOD, d)
        w_cat = jnp.concatenate([w_in[l], w_gate[l]], axis=1).astype(BF16)
        wa = w_attn_branch[l].astype(BF16)
        wf = w_fourier_branch[l].astype(BF16)
        wo = w_out[l].astype(BF16)
        wu = w_up[l].astype(BF16)
        wd = w_down[l].astype(BF16)
        sink = attn_sink[l].astype(F32)
        new_xs = []
        for x2d, (x_in, b_off) in zip(xs, groups):
            batch = x_in.shape[0]
            projg = _projection(x2d, mod, b_off, seq, g_mix[l], w_cat, b_gate[l],
                                in_width=in_width)
            attn = _attention(projg, slopes, sink, batch, seq,
                              attn_width=attn_width, kv_width=kv_width)
            four = _fourier(projg, chan_tab, pos_tab, batch, seq,
                            u_col0=attn_width + 2 * kv_width, f_width=f_width)
            merged = _merge(attn, four, projg, wa, wf, gate_col0=in_width)
            x1, h2 = _out_proj(merged, x2d, mod, b_off, seq, g_ffn[l], wo)
            act = _ffn_up(h2, wu)
            new_xs.append(_ffn_down(act, x1, mod, b_off, seq, g_final, wd, final_norm=last))
        xs = new_xs
    return tuple(x2d.reshape(x_in.shape) for x2d, (x_in, _) in zip(xs, groups))
```

```python
import functools
import math

import jax
import jax.numpy as jnp
from jax import lax
from jax.experimental import pallas as pl
from jax.experimental.pallas import tpu as pltpu

F32 = jnp.float32
BF16 = jnp.bfloat16

HEAD_DIM = 128
Q_PER_KV = 4
N_FOURIER_GROUPS = 4
WINDOW = 128
BLOCK = 128
N_MOD = 6
RMS_EPS = 1e-6
LOG2E = math.log2(math.e)

V7X_VMEM_LIMIT_BYTES = 60 * 1024 * 1024
BF16_SUBLANE_TILE = 16


def _params(*semantics):
    return pltpu.CompilerParams(dimension_semantics=semantics,
                                vmem_limit_bytes=V7X_VMEM_LIMIT_BYTES)


def _resident(block_shape, index_map):
    return pl.BlockSpec(block_shape, index_map, pipeline_mode=pl.Buffered(1))


def _rms_modulate(x, gain, scale, shift):
    y = x * lax.rsqrt(jnp.mean(x * x, axis=-1, keepdims=True) + RMS_EPS)
    return (y * gain) * (1.0 + scale) + shift


def _mod_kernel(c_ref, w_ref, b_ref, o_ref):
    c = c_ref[...]
    a = (c * jax.nn.sigmoid(c)).astype(BF16)
    o_ref[...] = jnp.dot(a, w_ref[...].astype(BF16), preferred_element_type=F32) + b_ref[...]


def _modulation(c, w_mod, b_mod, *, tn=512):
    bp, d = c.shape
    n = w_mod.shape[1]
    return pl.pallas_call(
        _mod_kernel,
        out_shape=jax.ShapeDtypeStruct((bp, n), F32),
        grid=(n // tn,),
        in_specs=[pl.BlockSpec((bp, d), lambda j: (0, 0)),
                  pl.BlockSpec((d, tn), lambda j: (0, j)),
                  pl.BlockSpec((1, tn), lambda j: (0, j))],
        out_specs=pl.BlockSpec((bp, tn), lambda j: (0, j)),
        name="modulation",
        compiler_params=_params("arbitrary"),
    )(c, w_mod, b_mod.reshape(1, n))


def _proj_kernel(x_ref, mod_ref, g_ref, w_ref, b_ref, o_ref, h_ref, *, n_in_tiles, row_chunk):
    j = pl.program_id(1)

    @pl.when(j == 0)
    def _():
        shift = mod_ref[0, 0:1, :]
        scale = mod_ref[0, 1:2, :]
        gain = g_ref[...]

        def body(r, carry):
            rows = pl.ds(pl.multiple_of(r * row_chunk, row_chunk), row_chunk)
            h_ref[rows, :] = _rms_modulate(x_ref[rows, :], gain, scale, shift).astype(BF16)
            return carry

        lax.fori_loop(0, x_ref.shape[0] // row_chunk, body, 0)

    @pl.when(j < n_in_tiles)
    def _():
        o_ref[...] = jnp.dot(h_ref[...], w_ref[...], preferred_element_type=F32).astype(BF16)

    @pl.when(j >= n_in_tiles)
    def _():
        acc = jnp.dot(h_ref[...], w_ref[...], preferred_element_type=F32)
        o_ref[...] = jax.nn.sigmoid(acc + b_ref[...]).astype(BF16)


def _projection(x2d, mod, b_off, seq, g_mix, w_cat, b_gate, *, in_width, tm=1024, tn=1024):
    m, d = x2d.shape
    n = w_cat.shape[1]
    n_in_tiles = in_width // tn
    tiles_per_seq = seq // tm
    kern = functools.partial(_proj_kernel, n_in_tiles=n_in_tiles, row_chunk=256)
    return pl.pallas_call(
        kern,
        out_shape=jax.ShapeDtypeStruct((m, n), BF16),
        grid=(m // tm, n // tn),
        in_specs=[pl.BlockSpec((tm, d), lambda i, j: (i, 0)),
                  pl.BlockSpec((1, N_MOD, d), lambda i, j: (b_off + i // tiles_per_seq, 0, 0)),
                  pl.BlockSpec((1, d), lambda i, j: (0, 0)),
                  pl.BlockSpec((d, tn), lambda i, j: (0, j)),
                  pl.BlockSpec((1, tn), lambda i, j: (0, jnp.maximum(j - n_in_tiles, 0)))],
        out_specs=pl.BlockSpec((tm, tn), lambda i, j: (i, j)),
        scratch_shapes=[pltpu.VMEM((tm, d), BF16)],
        name="projection",
        compiler_params=_params("parallel", "arbitrary"),
    )(x2d, mod, g_mix.reshape(1, d), w_cat, b_gate.reshape(1, -1))


def _attn_kernel(slopes_ref, sink_ref, q_ref, k_ref, v_ref, o_ref,
                 bias_ref, v1_ref, s_ref, p_ref, *, seq):
    kh = pl.program_id(1)
    n_chunks = 3
    span = n_chunks * BLOCK
    n_blocks = seq // BLOCK
    logit_scale = (HEAD_DIM ** -0.5) * LOG2E

    q_pos = lax.broadcasted_iota(jnp.int32, (BLOCK, BLOCK), 0)
    k_pos = lax.broadcasted_iota(jnp.int32, (BLOCK, BLOCK), 1)
    for d in range(-2, 3):
        rel = d * BLOCK + k_pos - q_pos
        valid = jnp.abs(rel) <= WINDOW
        dist = jnp.abs(rel).astype(F32)
        for g in range(Q_PER_KV):
            slope2 = slopes_ref[kh * Q_PER_KV + g] * LOG2E
            bias_ref[d + 2, g * BLOCK:(g + 1) * BLOCK, :] = jnp.where(valid, -slope2 * dist,
                                                                        -jnp.inf)
    sinks2 = [sink_ref[kh * Q_PER_KV + g] * LOG2E for g in range(Q_PER_KV)]
    v1_ref[:, :HEAD_DIM] = v_ref[...]
    v1_ref[:, HEAD_DIM:] = jnp.ones((seq, HEAD_DIM), BF16)

    def window(i):
        q0 = pl.multiple_of(i * BLOCK, BLOCK)
        k0 = pl.multiple_of(jnp.clip(q0 - BLOCK, 0, seq - span), BLOCK)
        return q0, k0

    def logits(i, slot):
        q0, k0 = window(i)
        first = lax.div(k0 - q0 + 2 * BLOCK, BLOCK)
        q = q_ref[pl.ds(q0, BLOCK), :]
        qs = jnp.concatenate([q[:, g * HEAD_DIM:(g + 1) * HEAD_DIM] for g in range(Q_PER_KV)],
                             axis=0)
        kw = k_ref[pl.ds(k0, span), :]
        s = lax.dot_general(qs, kw, (((1,), (1,)), ((), ())), preferred_element_type=F32)
        for t in range(n_chunks):
            cols = slice(t * BLOCK, (t + 1) * BLOCK)
            s_ref[slot, :, cols] = s[:, cols] * logit_scale + bias_ref[first + t]

    def softmax_values(i, slot):
        q0, k0 = window(i)
        row_max = []
        for g in range(Q_PER_KV):
            rows = slice(g * BLOCK, (g + 1) * BLOCK)
            a = s_ref[slot, rows, :]
            mx = jnp.maximum(jnp.max(a, axis=-1, keepdims=True), sinks2[g])
            p_ref[rows, :] = jnp.exp2(a - mx).astype(BF16)
            row_max.append(mx)
        r = jnp.dot(p_ref[...], v1_ref[pl.ds(k0, span), :], preferred_element_type=F32)
        for g in range(Q_PER_KV):
            rows = slice(g * BLOCK, (g + 1) * BLOCK)
            denom = r[rows, HEAD_DIM:] + jnp.exp2(sinks2[g] - row_max[g])
            o_ref[pl.ds(q0, BLOCK), g * HEAD_DIM:(g + 1) * HEAD_DIM] = (
                r[rows, :HEAD_DIM] / denom).astype(BF16)

    logits(0, 0)

    def body(it, carry):
        i = 2 * it
        logits(i + 1, 1)
        softmax_values(i, 0)
        logits(jnp.minimum(i + 2, n_blocks - 1), 0)
        softmax_values(i + 1, 1)
        return carry

    lax.fori_loop(0, n_blocks // 2, body, 0)


def _attention(proj, slopes, sink, batch, seq, *, attn_width, kv_width):
    m = proj.shape[0]
    n_kv = kv_width // HEAD_DIM
    gw = Q_PER_KV * HEAD_DIM
    k_col0 = attn_width // HEAD_DIM
    v_col0 = (attn_width + kv_width) // HEAD_DIM
    smem = pl.BlockSpec(memory_space=pltpu.SMEM)
    return pl.pallas_call(
        functools.partial(_attn_kernel, seq=seq),
        out_shape=jax.ShapeDtypeStruct((m, attn_width), BF16),
        grid=(batch, n_kv),
        in_specs=[smem, smem,
                  pl.BlockSpec((seq, gw), lambda b, h: (b, h)),
                  pl.BlockSpec((seq, HEAD_DIM), lambda b, h: (b, k_col0 + h)),
                  pl.BlockSpec((seq, HEAD_DIM), lambda b, h: (b, v_col0 + h))],
        out_specs=pl.BlockSpec((seq, gw), lambda b, h: (b, h)),
        scratch_shapes=[pltpu.VMEM((5, Q_PER_KV * BLOCK, BLOCK), F32),
                        pltpu.VMEM((seq, 2 * HEAD_DIM), BF16),
                        pltpu.VMEM((2, Q_PER_KV * BLOCK, 3 * BLOCK), F32),
                        pltpu.VMEM((Q_PER_KV * BLOCK, 3 * BLOCK), BF16)],
        name="attention",
        compiler_params=_params("parallel", "arbitrary"),
    )(slopes, sink, proj, proj, proj)


def _fourier_kernel(u_ref, cd_ref, cs_ref, o_ref, ab_ref, *, seq, gd):
    @pl.when(pl.program_id(1) == 0)
    def _():
        for g in range(N_FOURIER_GROUPS):
            cols = slice(g * gd, (g + 1) * gd)
            t = jnp.dot(u_ref[:, cols], cd_ref[...], preferred_element_type=F32)
            ab_ref[0:seq, cols] = t[:, :gd].astype(BF16)
            ab_ref[seq:2 * seq, cols] = t[:, gd:].astype(BF16)

    o_ref[...] = jnp.dot(cs_ref[...], ab_ref[...], preferred_element_type=F32).astype(BF16)


def _dft_tables(seq, gd):
    def table(n):
        idx = jnp.arange(n, dtype=jnp.int32)
        ang = ((idx[:, None] * idx[None, :]) % n).astype(F32) * (2.0 * jnp.pi / n)
        return jnp.cos(ang), jnp.sin(ang)
    cd, sd = table(gd)
    cs, ss = table(seq)
    chan = jnp.concatenate([cd, sd], axis=1) * (gd ** -0.5)
    pos = jnp.concatenate([cs, -ss], axis=1) * (seq ** -0.5)
    return chan.astype(BF16), pos.astype(BF16)


def _fourier(proj, chan_tab, pos_tab, batch, seq, *, u_col0, f_width, tm=512):
    m = proj.shape[0]
    gd = f_width // N_FOURIER_GROUPS
    return pl.pallas_call(
        functools.partial(_fourier_kernel, seq=seq, gd=gd),
        out_shape=jax.ShapeDtypeStruct((m, f_width), BF16),
        grid=(batch, seq // tm),
        in_specs=[pl.BlockSpec((seq, f_width), lambda b, i: (b, u_col0 // f_width)),
                  _resident((gd, 2 * gd), lambda b, i: (0, 0)),
                  pl.BlockSpec((tm, 2 * seq), lambda b, i: (i, 0))],
        out_specs=pl.BlockSpec((tm, f_width), lambda b, i: (b * (seq // tm) + i, 0)),
        scratch_shapes=[pltpu.VMEM((2 * seq, f_width), BF16)],
        name="fourier",
        compiler_params=_params("parallel", "arbitrary"),
    )(proj, chan_tab, pos_tab)


def _merge_kernel(a_ref, f_ref, ga_ref, gf_ref, wa_ref, wf_ref, o_ref):
    ta = jnp.dot(a_ref[...], wa_ref[...], preferred_element_type=F32)
    tf = jnp.dot(f_ref[...], wf_ref[...], preferred_element_type=F32)
    o_ref[...] = (ga_ref[...].astype(F32) * ta + gf_ref[...].astype(F32) * tf).astype(BF16)


def _merge(attn, four, projg, wa, wf, *, gate_col0, tm=1024, tn=1024):
    m, aw = attn.shape
    fw = four.shape[1]
    d = wa.shape[1]
    ga0 = gate_col0 // tn
    gf0 = (gate_col0 + d) // tn
    return pl.pallas_call(
        _merge_kernel,
        out_shape=jax.ShapeDtypeStruct((m, d), BF16),
        grid=(m // tm, d // tn),
        in_specs=[pl.BlockSpec((tm, aw), lambda i, j: (i, 0)),
                  pl.BlockSpec((tm, fw), lambda i, j: (i, 0)),
                  pl.BlockSpec((tm, tn), lambda i, j: (i, ga0 + j)),
                  pl.BlockSpec((tm, tn), lambda i, j: (i, gf0 + j)),
                  pl.BlockSpec((aw, tn), lambda i, j: (0, j)),
                  pl.BlockSpec((fw, tn), lambda i, j: (0, j))],
        out_specs=pl.BlockSpec((tm, tn), lambda i, j: (i, j)),
        name="merge",
        compiler_params=_params("parallel", "arbitrary"),
    )(attn, four, projg, projg, wa, wf)


def _out_kernel(m_ref, x_ref, mod_ref, g_ref, wo_ref, x1_ref, h2_ref):
    out = jnp.dot(m_ref[...], wo_ref[...], preferred_element_type=F32)
    x1 = x_ref[...] + mod_ref[0, 2:3, :] * out
    x1_ref[...] = x1
    h2 = _rms_modulate(x1, g_ref[...], mod_ref[0, 4:5, :], mod_ref[0, 3:4, :])
    h2_ref[...] = h2.astype(BF16)


def _out_proj(merged, x2d, mod, b_off, seq, g_ffn, wo, *, tm=512):
    m, d = x2d.shape
    tiles_per_seq = seq // tm
    return pl.pallas_call(
        _out_kernel,
        out_shape=(jax.ShapeDtypeStruct((m, d), F32), jax.ShapeDtypeStruct((m, d), BF16)),
        grid=(m // tm,),
        in_specs=[pl.BlockSpec((tm, d), lambda i: (i, 0)),
                  pl.BlockSpec((tm, d), lambda i: (i, 0)),
                  pl.BlockSpec((1, N_MOD, d), lambda i: (b_off + i // tiles_per_seq, 0, 0)),
                  pl.BlockSpec((1, d), lambda i: (0, 0)),
                  _resident((d, d), lambda i: (0, 0))],
        out_specs=(pl.BlockSpec((tm, d), lambda i: (i, 0)),
                   pl.BlockSpec((tm, d), lambda i: (i, 0))),
        name="out_proj",
        compiler_params=_params("parallel"),
    )(merged, x2d, mod, g_ffn.reshape(1, d), wo)


def _ffn_up_kernel(h_ref, wg_ref, wu_ref, o_ref):
    h = h_ref[...]
    gt = jnp.dot(h, wg_ref[...], preferred_element_type=F32)
    up = jnp.dot(h, wu_ref[...], preferred_element_type=F32)
    o_ref[...] = (gt * jax.nn.sigmoid(gt) * up).astype(BF16)


def _ffn_up(h2, w_up, *, tm=1024, tn=1408):
    m, d = h2.shape
    dff = w_up.shape[1] // 2
    nj = dff // tn
    return pl.pallas_call(
        _ffn_up_kernel,
        out_shape=jax.ShapeDtypeStruct((m, dff), BF16),
        grid=(nj, m // tm),
        in_specs=[pl.BlockSpec((tm, d), lambda j, i: (i, 0)),
                  _resident((d, tn), lambda j, i: (0, j)),
                  _resident((d, tn), lambda j, i: (0, nj + j))],
        out_specs=pl.BlockSpec((tm, tn), lambda j, i: (i, j)),
        name="ffn_up",
        compiler_params=_params("arbitrary", "parallel"),
    )(h2, w_up, w_up)


def _ffn_down_kernel(a_ref, x1_ref, mod_ref, g_ref, wd_ref, y_ref, *, final_norm):
    out = jnp.dot(a_ref[...], wd_ref[...], preferred_element_type=F32)
    x2 = x1_ref[...] + mod_ref[0, 5:6, :] * out
    if final_norm:
        y = x2 * lax.rsqrt(jnp.mean(x2 * x2, axis=-1, keepdims=True) + RMS_EPS)
        x2 = y * g_ref[...]
    y_ref[...] = x2


def _ffn_down(act, x1, mod, b_off, seq, g_final, wd, *, final_norm, tm=256):
    m, d = x1.shape
    dff = act.shape[1]
    tiles_per_seq = seq // tm
    return pl.pallas_call(
        functools.partial(_ffn_down_kernel, final_norm=final_norm),
        out_shape=jax.ShapeDtypeStruct((m, d), F32),
        grid=(m // tm,),
        in_specs=[pl.BlockSpec((tm, dff), lambda i: (i, 0)),
                  pl.BlockSpec((tm, d), lambda i: (i, 0)),
                  pl.BlockSpec((1, N_MOD, d), lambda i: (b_off + i // tiles_per_seq, 0, 0)),
                  pl.BlockSpec((1, d), lambda i: (0, 0)),
                  _resident((dff, d), lambda i: (0, 0))],
        out_specs=pl.BlockSpec((tm, d), lambda i: (i, 0)),
        name="ffn_down",
        compiler_params=_params("parallel"),
    )(act, x1, mod, g_final.reshape(1, d), wd)


def kernel(x_prompt, x_sample, c_prompt, c_sample, w_mod, b_mod, g_mix, w_in, attn_sink,
           w_attn_branch, w_fourier_branch, w_gate, b_gate, w_out, g_ffn, w_up, w_down,
           g_final):
    depth, d, in_width = w_in.shape
    attn_width = w_attn_branch.shape[1]
    f_width = w_fourier_branch.shape[1]
    kv_width = (in_width - attn_width - f_width) // 2
    n_heads = attn_width // HEAD_DIM
    groups = [(x_prompt, 0), (x_sample, c_prompt.shape[0])]
    seq = x_prompt.shape[1]
    assert x_sample.shape[1] == seq and seq % BLOCK == 0 and seq >= 3 * BLOCK
    assert (attn_width + 2 * kv_width) % f_width == 0

    c_all = jnp.concatenate([c_prompt, c_sample], axis=0)
    pad = -c_all.shape[0] % BF16_SUBLANE_TILE
    c_all = jnp.pad(c_all, ((0, pad), (0, 0)))

    heads = jnp.arange(1, n_heads + 1, dtype=F32)
    slopes = jnp.exp2(-8.0 * heads / n_heads)
    chan_tab, pos_tab = _dft_tables(seq, f_width // N_FOURIER_GROUPS)

    xs = [x.reshape(-1, d) for x, _ in groups]
    for l in range(depth):
        last = l == depth - 1
        mod = _modulation(c_all, w_mod[l], b_mod[l]).reshape(-1, N_MOD, d)
        w_cat = jnp.concatenate([w_in[l], w_gate[l]], axis=1).astype(BF16)
        wa = w_attn_branch[l].astype(BF16)
        wf = w_fourier_branch[l].astype(BF16)
        wo = w_out[l].astype(BF16)
        wu = w_up[l].astype(BF16)
        wd = w_down[l].astype(BF16)
        sink = attn_sink[l].astype(F32)
        new_xs = []
        for x2d, (x_in, b_off) in zip(xs, groups):
            batch = x_in.shape[0]
            projg = _projection(x2d, mod, b_off, seq, g_mix[l], w_cat, b_gate[l],
                                in_width=in_width)
            attn = _attention(projg, slopes, sink, batch, seq,
                              attn_width=attn_width, kv_width=kv_width)
            four = _fourier(projg, chan_tab, pos_tab, batch, seq,
                            u_col0=attn_width + 2 * kv_width, f_width=f_width)
            merged = _merge(attn, four, projg, wa, wf, gate_col0=in_width)
            x1, h2 = _out_proj(merged, x2d, mod, b_off, seq, g_ffn[l], wo)
            act = _ffn_up(h2, wu)
            new_xs.append(_ffn_down(act, x1, mod, b_off, seq, g_final, wd, final_norm=last))
        xs = new_xs
    return tuple(x2d.reshape(x_in.shape) for x2d, (x_in, _) in zip(xs, groups))
```

```python
import functools
import math

import jax
import jax.numpy as jnp
from jax import lax
from jax.experimental import pallas as pl
from jax.experimental.pallas import tpu as pltpu

F32 = jnp.float32
BF16 = jnp.bfloat16

HEAD_DIM = 128
Q_PER_KV = 4
N_FOURIER_GROUPS = 4
WINDOW = 128
BLOCK = 128
N_MOD = 6
RMS_EPS = 1e-6
LOG2E = math.log2(math.e)

V7X_VMEM_LIMIT_BYTES = 60 * 1024 * 1024
BF16_SUBLANE_TILE = 16


def _params(*semantics):
    return pltpu.CompilerParams(dimension_semantics=semantics,
                                vmem_limit_bytes=V7X_VMEM_LIMIT_BYTES)


def _resident(block_shape, index_map):
    return pl.BlockSpec(block_shape, index_map, pipeline_mode=pl.Buffered(1))


def _rms_modulate(x, gain, scale, shift):
    y = x * lax.rsqrt(jnp.mean(x * x, axis=-1, keepdims=True) + RMS_EPS)
    return (y * gain) * (1.0 + scale) + shift


def _mod_kernel(c_ref, w_ref, b_ref, o_ref):
    c = c_ref[...]
    a = (c * jax.nn.sigmoid(c)).astype(BF16)
    o_ref[...] = jnp.dot(a, w_ref[...].astype(BF16), preferred_element_type=F32) + b_ref[...]


def _modulation(c, w_mod, b_mod, *, tn=512):
    bp, d = c.shape
    n = w_mod.shape[1]
    return pl.pallas_call(
        _mod_kernel,
        out_shape=jax.ShapeDtypeStruct((bp, n), F32),
        grid=(n // tn,),
        in_specs=[pl.BlockSpec((bp, d), lambda j: (0, 0)),
                  pl.BlockSpec((d, tn), lambda j: (0, j)),
                  pl.BlockSpec((1, tn), lambda j: (0, j))],
        out_specs=pl.BlockSpec((bp, tn), lambda j: (0, j)),
        name="modulation",
        compiler_params=_params("arbitrary"),
    )(c, w_mod, b_mod.reshape(1, n))


def _proj_kernel(x0_ref, xn_ref, mod0_ref, modn_ref, g_ref, w_ref, b_ref, o_ref,
                 h_even_ref, h_odd_ref, *, n_in_tiles, n_col_tiles, row_chunk):
    i = pl.program_id(0)
    j = pl.program_id(1)
    odd = lax.rem(i, 2) == 1
    gain = g_ref[...]
    tm = xn_ref.shape[0]
    rows_per_step = tm // n_col_tiles

    def norm_rows(x_ref, mod_ref, dst_ref, row0, n_rows):
        rows = pl.ds(pl.multiple_of(row0, n_rows), n_rows)
        h = _rms_modulate(x_ref[rows, :], gain, mod_ref[0, 1:2, :], mod_ref[0, 0:1, :])
        dst_ref[rows, :] = h.astype(BF16)

    @pl.when((i == 0) & (j == 0))
    def _():
        def body(r, carry):
            norm_rows(x0_ref, mod0_ref, h_even_ref, r * row_chunk, row_chunk)
            return carry

        lax.fori_loop(0, tm // row_chunk, body, 0)

    def column_step(cur_ref, nxt_ref, gated):
        norm_rows(xn_ref, modn_ref, nxt_ref, j * rows_per_step, rows_per_step)
        acc = jnp.dot(cur_ref[...], w_ref[...], preferred_element_type=F32)
        if gated:
            acc = 0.5 * jnp.tanh(0.5 * (acc + b_ref[...])) + 0.5
        o_ref[...] = acc.astype(BF16)

    for is_odd, cur_ref, nxt_ref in ((False, h_even_ref, h_odd_ref),
                                     (True, h_odd_ref, h_even_ref)):
        for gated in (False, True):
            pl.when((odd == is_odd) & ((j >= n_in_tiles) == gated))(
                functools.partial(column_step, cur_ref, nxt_ref, gated))


def _projection(x2d, mod, b_off, seq, g_mix, w_cat, b_gate, *, in_width, tm=1024, tn=1024):
    m, d = x2d.shape
    n = w_cat.shape[1]
    n_in_tiles = in_width // tn
    n_col_tiles = n // tn
    tiles_per_seq = seq // tm
    last = m // tm - 1
    kern = functools.partial(_proj_kernel, n_in_tiles=n_in_tiles, n_col_tiles=n_col_tiles,
                             row_chunk=256)
    nxt = lambda i: jnp.minimum(i + 1, last)
    return pl.pallas_call(
        kern,
        out_shape=jax.ShapeDtypeStruct((m, n), BF16),
        grid=(m // tm, n_col_tiles),
        in_specs=[_resident((tm, d), lambda i, j: (0, 0)),
                  pl.BlockSpec((tm, d), lambda i, j: (nxt(i), 0)),
                  _resident((1, N_MOD, d), lambda i, j: (b_off, 0, 0)),
                  pl.BlockSpec((1, N_MOD, d),
                               lambda i, j: (b_off + nxt(i) // tiles_per_seq, 0, 0)),
                  pl.BlockSpec((1, d), lambda i, j: (0, 0)),
                  pl.BlockSpec((d, tn), lambda i, j: (0, j)),
                  pl.BlockSpec((1, tn), lambda i, j: (0, jnp.maximum(j - n_in_tiles, 0)))],
        out_specs=pl.BlockSpec((tm, tn), lambda i, j: (i, j)),
        scratch_shapes=[pltpu.VMEM((tm, d), BF16), pltpu.VMEM((tm, d), BF16)],
        name="projection",
        compiler_params=_params("arbitrary", "arbitrary"),
    )(x2d, x2d, mod, mod, g_mix.reshape(1, d), w_cat, b_gate.reshape(1, -1))


def _attn_kernel(slopes_ref, sink_ref, q_ref, k_ref, v_ref, o_ref,
                 bias_ref, v1_ref, s_ref, p_ref, *, seq):
    kh = pl.program_id(1)
    n_chunks = 3
    span = n_chunks * BLOCK
    n_blocks = seq // BLOCK
    logit_scale = (HEAD_DIM ** -0.5) * LOG2E

    q_pos = lax.broadcasted_iota(jnp.int32, (BLOCK, BLOCK), 0)
    k_pos = lax.broadcasted_iota(jnp.int32, (BLOCK, BLOCK), 1)
    for d in range(-2, 3):
        rel = d * BLOCK + k_pos - q_pos
        valid = jnp.abs(rel) <= WINDOW
        dist = jnp.abs(rel).astype(F32)
        for g in range(Q_PER_KV):
            slope2 = slopes_ref[kh * Q_PER_KV + g] * LOG2E
            bias_ref[d + 2, g * BLOCK:(g + 1) * BLOCK, :] = jnp.where(valid, -slope2 * dist,
                                                                        -jnp.inf)
    sinks2 = [sink_ref[kh * Q_PER_KV + g] * LOG2E for g in range(Q_PER_KV)]
    v1_ref[:, :HEAD_DIM] = v_ref[...]
    v1_ref[:, HEAD_DIM:] = jnp.ones((seq, HEAD_DIM), BF16)

    def window(i):
        q0 = pl.multiple_of(i * BLOCK, BLOCK)
        k0 = pl.multiple_of(jnp.clip(q0 - BLOCK, 0, seq - span), BLOCK)
        return q0, k0

    def logits(i, slot):
        q0, k0 = window(i)
        first = lax.div(k0 - q0 + 2 * BLOCK, BLOCK)
        q = q_ref[pl.ds(q0, BLOCK), :]
        qs = jnp.concatenate([q[:, g * HEAD_DIM:(g + 1) * HEAD_DIM] for g in range(Q_PER_KV)],
                             axis=0)
        kw = k_ref[pl.ds(k0, span), :]
        s = lax.dot_general(qs, kw, (((1,), (1,)), ((), ())), preferred_element_type=F32)
        for t in range(n_chunks):
            cols = slice(t * BLOCK, (t + 1) * BLOCK)
            s_ref[slot, :, cols] = s[:, cols] * logit_scale + bias_ref[first + t]

    def softmax_values(i, slot):
        q0, k0 = window(i)
        row_max = []
        for g in range(Q_PER_KV):
            rows = slice(g * BLOCK, (g + 1) * BLOCK)
            a = s_ref[slot, rows, :]
            mx = jnp.maximum(jnp.max(a, axis=-1, keepdims=True), sinks2[g])
            p_ref[rows, :] = jnp.exp2(a - mx).astype(BF16)
            row_max.append(mx)
        r = jnp.dot(p_ref[...], v1_ref[pl.ds(k0, span), :], preferred_element_type=F32)
        for g in range(Q_PER_KV):
            rows = slice(g * BLOCK, (g + 1) * BLOCK)
            denom = r[rows, HEAD_DIM:] + jnp.exp2(sinks2[g] - row_max[g])
            o_ref[pl.ds(q0, BLOCK), g * HEAD_DIM:(g + 1) * HEAD_DIM] = (
                r[rows, :HEAD_DIM] / denom).astype(BF16)

    logits(0, 0)

    def body(it, carry):
        i = 2 * it
        logits(i + 1, 1)
        softmax_values(i, 0)
        logits(jnp.minimum(i + 2, n_blocks - 1), 0)
        softmax_values(i + 1, 1)
        return carry

    lax.fori_loop(0, n_blocks // 2, body, 0)


def _attention(proj, slopes, sink, batch, seq, *, attn_width, kv_width):
    m = proj.shape[0]
    n_kv = kv_width // HEAD_DIM
    gw = Q_PER_KV * HEAD_DIM
    k_col0 = attn_width // HEAD_DIM
    v_col0 = (attn_width + kv_width) // HEAD_DIM
    smem = pl.BlockSpec(memory_space=pltpu.SMEM)
    return pl.pallas_call(
        functools.partial(_attn_kernel, seq=seq),
        out_shape=jax.ShapeDtypeStruct((m, attn_width), BF16),
        grid=(batch, n_kv),
        in_specs=[smem, smem,
                  pl.BlockSpec((seq, gw), lambda b, h: (b, h)),
                  pl.BlockSpec((seq, HEAD_DIM), lambda b, h: (b, k_col0 + h)),
                  pl.BlockSpec((seq, HEAD_DIM), lambda b, h: (b, v_col0 + h))],
        out_specs=pl.BlockSpec((seq, gw), lambda b, h: (b, h)),
        scratch_shapes=[pltpu.VMEM((5, Q_PER_KV * BLOCK, BLOCK), F32),
                        pltpu.VMEM((seq, 2 * HEAD_DIM), BF16),
                        pltpu.VMEM((2, Q_PER_KV * BLOCK, 3 * BLOCK), F32),
                        pltpu.VMEM((Q_PER_KV * BLOCK, 3 * BLOCK), BF16)],
        name="attention",
        compiler_params=_params("parallel", "arbitrary"),
    )(slopes, sink, proj, proj, proj)


def _fourier_kernel(u_ref, cd_ref, sd_ref, ch_ref, sh_ref, pm_ref, o_ref,
                    ue_ref, uo_ref, ae_ref, bo_ref, y_ref, *, seq, gd):
    half = seq // 2
    n_blocks = half // BLOCK
    pm = pm_ref[...]

    def reversed_block(src_ref, b, end):
        lo = end - (b + 1) * BLOCK
        if b == 0:
            return jnp.dot(pm[:, :BLOCK], src_ref[lo:end, :], preferred_element_type=F32)
        return jnp.dot(pm, src_ref[lo:lo + 2 * BLOCK, :], preferred_element_type=F32)

    for b in range(n_blocks):
        rows = slice(b * BLOCK, (b + 1) * BLOCK)
        x = u_ref[rows, :].astype(F32)
        r = reversed_block(u_ref, b, seq)
        ue_ref[rows, :] = (x + r).astype(BF16)
        uo_ref[rows, :] = (x - r).astype(BF16)

    mid = []
    for g in range(N_FOURIER_GROUPS):
        cols = slice(g * gd, (g + 1) * gd)
        ae_ref[:, cols] = jnp.dot(ue_ref[:, cols], cd_ref[...],
                                  preferred_element_type=F32).astype(BF16)
        bo_ref[:, cols] = jnp.dot(uo_ref[:, cols], sd_ref[...],
                                  preferred_element_type=F32).astype(BF16)
        mid.append(jnp.dot(u_ref[half:half + BF16_SUBLANE_TILE, cols], cd_ref[...],
                           preferred_element_type=F32)[0:1])
    a_mid = jnp.concatenate(mid, axis=1) * (seq ** -0.5)

    p = jnp.dot(ch_ref[...], ae_ref[...], preferred_element_type=F32)
    q = jnp.dot(sh_ref[...], bo_ref[...], preferred_element_type=F32)
    k_idx = lax.broadcasted_iota(jnp.int32, (half, 1), 0)
    sign = 1.0 - 2.0 * (k_idx & 1).astype(F32)
    base = p[:half] + sign * a_mid
    o_ref[0:half, :] = (base - q).astype(BF16)
    y_ref[...] = (base + q).astype(BF16)
    z_mid = p[half:half + 1] + a_mid

    for b in range(n_blocks):
        r = reversed_block(y_ref, b, half)
        if b == 0:
            first_row = lax.broadcasted_iota(jnp.int32, (BLOCK, 1), 0) == 0
            r = jnp.where(first_row, z_mid, r)
        o_ref[half + b * BLOCK:half + (b + 1) * BLOCK, :] = r.astype(BF16)


def _dft_tables(seq, gd):
    half = seq // 2

    def angles(rows, cols, n):
        k = jnp.arange(rows, dtype=jnp.int32)[:, None]
        s = jnp.arange(cols, dtype=jnp.int32)[None, :]
        return ((k * s) % n).astype(F32) * (2.0 * jnp.pi / n)

    ang_c = angles(gd, gd, gd)
    cd = jnp.cos(ang_c) * (gd ** -0.5)
    sd = jnp.sin(ang_c) * (gd ** -0.5)
    pad_rows = half + BF16_SUBLANE_TILE
    ang_p = angles(pad_rows, half, seq)
    live = jnp.arange(pad_rows)[:, None] <= half
    ch = jnp.where(live, jnp.cos(ang_p), 0.0) * (seq ** -0.5)
    sh = jnp.sin(ang_p[:half]) * (seq ** -0.5)
    a = jnp.arange(BLOCK)[:, None]
    j = jnp.arange(2 * BLOCK)[None, :]
    pm = jnp.where(a == 0, j == BLOCK, j == BLOCK - a)
    return tuple(t.astype(BF16) for t in (cd, sd, ch, sh, pm))


def _fourier(proj, tables, batch, seq, *, u_col0, f_width):
    m = proj.shape[0]
    gd = f_width // N_FOURIER_GROUPS
    half = seq // 2
    cd, sd, ch, sh, pm = tables
    const = lambda shape: _resident(shape, lambda b: (0, 0))
    return pl.pallas_call(
        functools.partial(_fourier_kernel, seq=seq, gd=gd),
        out_shape=jax.ShapeDtypeStruct((m, f_width), BF16),
        grid=(batch,),
        in_specs=[pl.BlockSpec((seq, f_width), lambda b: (b, u_col0 // f_width)),
                  const(cd.shape), const(sd.shape), const(ch.shape), const(sh.shape),
                  const(pm.shape)],
        out_specs=pl.BlockSpec((seq, f_width), lambda b: (b, 0)),
        scratch_shapes=[pltpu.VMEM((half, f_width), BF16) for _ in range(5)],
        name="fourier",
        compiler_params=_params("parallel"),
    )(proj, cd, sd, ch, sh, pm)


def _merge_kernel(a_ref, f_ref, ga_ref, gf_ref, wa_ref, wf_ref, o_ref):
    ta = jnp.dot(a_ref[...], wa_ref[...], preferred_element_type=F32)
    tf = jnp.dot(f_ref[...], wf_ref[...], preferred_element_type=F32)
    o_ref[...] = (ga_ref[...].astype(F32) * ta + gf_ref[...].astype(F32) * tf).astype(BF16)


def _merge(attn, four, projg, wa, wf, *, gate_col0, tm=1024, tn=1024):
    m, aw = attn.shape
    fw = four.shape[1]
    d = wa.shape[1]
    ga0 = gate_col0 // tn
    gf0 = (gate_col0 + d) // tn
    return pl.pallas_call(
        _merge_kernel,
        out_shape=jax.ShapeDtypeStruct((m, d), BF16),
        grid=(m // tm, d // tn),
        in_specs=[pl.BlockSpec((tm, aw), lambda i, j: (i, 0)),
                  pl.BlockSpec((tm, fw), lambda i, j: (i, 0)),
                  pl.BlockSpec((tm, tn), lambda i, j: (i, ga0 + j)),
                  pl.BlockSpec((tm, tn), lambda i, j: (i, gf0 + j)),
                  pl.BlockSpec((aw, tn), lambda i, j: (0, j)),
                  pl.BlockSpec((fw, tn), lambda i, j: (0, j))],
        out_specs=pl.BlockSpec((tm, tn), lambda i, j: (i, j)),
        name="merge",
        compiler_params=_params("parallel", "arbitrary"),
    )(attn, four, projg, projg, wa, wf)


def _out_kernel(m_ref, x_ref, mod_ref, g_ref, wo_ref, x1_ref, h2_ref):
    out = jnp.dot(m_ref[...], wo_ref[...], preferred_element_type=F32)
    x1 = x_ref[...] + mod_ref[0, 2:3, :] * out
    x1_ref[...] = x1
    h2 = _rms_modulate(x1, g_ref[...], mod_ref[0, 4:5, :], mod_ref[0, 3:4, :])
    h2_ref[...] = h2.astype(BF16)


def _out_proj(merged, x2d, mod, b_off, seq, g_ffn, wo, *, tm=512):
    m, d = x2d.shape
    tiles_per_seq = seq // tm
    return pl.pallas_call(
        _out_kernel,
        out_shape=(jax.ShapeDtypeStruct((m, d), F32), jax.ShapeDtypeStruct((m, d), BF16)),
        grid=(m // tm,),
        in_specs=[pl.BlockSpec((tm, d), lambda i: (i, 0)),
                  pl.BlockSpec((tm, d), lambda i: (i, 0)),
                  pl.BlockSpec((1, N_MOD, d), lambda i: (b_off + i // tiles_per_seq, 0, 0)),
                  pl.BlockSpec((1, d), lambda i: (0, 0)),
                  _resident((d, d), lambda i: (0, 0))],
        out_specs=(pl.BlockSpec((tm, d), lambda i: (i, 0)),
                   pl.BlockSpec((tm, d), lambda i: (i, 0))),
        name="out_proj",
        compiler_params=_params("parallel"),
    )(merged, x2d, mod, g_ffn.reshape(1, d), wo)


def _ffn_up_kernel(h_ref, wg_ref, wu_ref, o_ref):
    h = h_ref[...]
    gt = jnp.dot(h, wg_ref[...], preferred_element_type=F32)
    up = jnp.dot(h, wu_ref[...], preferred_element_type=F32)
    o_ref[...] = (gt * jax.nn.sigmoid(gt) * up).astype(BF16)


def _ffn_up(h2, w_up, *, tm=512, tn=2816):
    m, d = h2.shape
    dff = w_up.shape[1] // 2
    nj = dff // tn
    return pl.pallas_call(
        _ffn_up_kernel,
        out_shape=jax.ShapeDtypeStruct((m, dff), BF16),
        grid=(nj, m // tm),
        in_specs=[pl.BlockSpec((tm, d), lambda j, i: (i, 0)),
                  _resident((d, tn), lambda j, i: (0, j)),
                  _resident((d, tn), lambda j, i: (0, nj + j))],
        out_specs=pl.BlockSpec((tm, tn), lambda j, i: (i, j)),
        name="ffn_up",
        compiler_params=_params("arbitrary", "parallel"),
    )(h2, w_up, w_up)


def _ffn_down_kernel(a_ref, x1_ref, mod_ref, g_ref, wd_ref, y_ref, *, final_norm):
    out = jnp.dot(a_ref[...], wd_ref[...], preferred_element_type=F32)
    x2 = x1_ref[...] + mod_ref[0, 5:6, :] * out
    if final_norm:
        y = x2 * lax.rsqrt(jnp.mean(x2 * x2, axis=-1, keepdims=True) + RMS_EPS)
        x2 = y * g_ref[...]
    y_ref[...] = x2


def _ffn_down(act, x1, mod, b_off, seq, g_final, wd, *, final_norm, tm=256):
    m, d = x1.shape
    dff = act.shape[1]
    tiles_per_seq = seq // tm
    return pl.pallas_call(
        functools.partial(_ffn_down_kernel, final_norm=final_norm),
        out_shape=jax.ShapeDtypeStruct((m, d), F32),
        grid=(m // tm,),
        in_specs=[pl.BlockSpec((tm, dff), lambda i: (i, 0)),
                  pl.BlockSpec((tm, d), lambda i: (i, 0)),
                  pl.BlockSpec((1, N_MOD, d), lambda i: (b_off + i // tiles_per_seq, 0, 0)),
                  pl.BlockSpec((1, d), lambda i: (0, 0)),
                  _resident((dff, d), lambda i: (0, 0))],
        out_specs=pl.BlockSpec((tm, d), lambda i: (i, 0)),
        name="ffn_down",
        compiler_params=_params("parallel"),
    )(act, x1, mod, g_final.reshape(1, d), wd)


def kernel(x_prompt, x_sample, c_prompt, c_sample, w_mod, b_mod, g_mix, w_in, attn_sink,
           w_attn_branch, w_fourier_branch, w_gate, b_gate, w_out, g_ffn, w_up, w_down,
           g_final):
    depth, d, in_width = w_in.shape
    attn_width = w_attn_branch.shape[1]
    f_width = w_fourier_branch.shape[1]
    kv_width = (in_width - attn_width - f_width) // 2
    n_heads = attn_width // HEAD_DIM
    groups = [(x_prompt, 0), (x_sample, c_prompt.shape[0])]
    seq = x_prompt.shape[1]
    assert x_sample.shape[1] == seq and seq % (4 * BLOCK) == 0
    assert (attn_width + 2 * kv_width) % f_width == 0

    c_all = jnp.concatenate([c_prompt, c_sample], axis=0)
    pad = -c_all.shape[0] % BF16_SUBLANE_TILE
    c_all = jnp.pad(c_all, ((0, pad), (0, 0)))

    heads = jnp.arange(1, n_heads + 1, dtype=F32)
    slopes = jnp.exp2(-8.0 * heads / n_heads)
    dft_tables = _dft_tables(seq, f_width // N_FOURIER_GROUPS)

    xs = [x.reshape(-1, d) for x, _ in groups]
    for l in range(depth):
        last = l == depth - 1
        mod = _modulation(c_all, w_mod[l], b_mod[l]).reshape(-1, N_MOD, d)
        w_cat = jnp.concatenate([w_in[l], w_gate[l]], axis=1).astype(BF16)
        wa = w_attn_branch[l].astype(BF16)
        wf = w_fourier_branch[l].astype(BF16)
        wo = w_out[l].astype(BF16)
        wu = w_up[l].astype(BF16)
        wd = w_down[l].astype(BF16)
        sink = attn_sink[l].astype(F32)
        new_xs = []
        for x2d, (x_in, b_off) in zip(xs, groups):
            batch = x_in.shape[0]
            projg = _projection(x2d, mod, b_off, seq, g_mix[l], w_cat, b_gate[l],
                                in_width=in_width)
            attn = _attention(projg, slopes, sink, batch, seq,
                              attn_width=attn_width, kv_width=kv_width)
            four = _fourier(projg, dft_tables, batch, seq,
                            u_col0=attn_width + 2 * kv_width, f_width=f_width)
            merged = _merge(attn, four, projg, wa, wf, gate_col0=in_width)
            x1, h2 = _out_proj(merged, x2d, mod, b_off, seq, g_ffn[l], wo)
            act = _ffn_up(h2, wu)
            new_xs.append(_ffn_down(act, x1, mod, b_off, seq, g_final, wd, final_norm=last))
        xs = new_xs
    return tuple(x2d.reshape(x_in.shape) for x2d, (x_in, _) in zip(xs, groups))
```

```python
import functools
import math

import jax
import jax.numpy as jnp
from jax import lax
from jax.experimental import pallas as pl
from jax.experimental.pallas import tpu as pltpu

F32 = jnp.float32
BF16 = jnp.bfloat16

HEAD_DIM = 128
Q_PER_KV = 4
N_FOURIER_GROUPS = 4
WINDOW = 128
BLOCK = 128
N_MOD = 6
RMS_EPS = 1e-6
LOG2E = math.log2(math.e)

V7X_VMEM_LIMIT_BYTES = 60 * 1024 * 1024
BF16_SUBLANE_TILE = 16


def _params(*semantics):
    return pltpu.CompilerParams(dimension_semantics=semantics,
                                vmem_limit_bytes=V7X_VMEM_LIMIT_BYTES)


def _resident(block_shape, index_map):
    return pl.BlockSpec(block_shape, index_map, pipeline_mode=pl.Buffered(1))


def _rms_modulate(x, gain, scale, shift):
    y = x * lax.rsqrt(jnp.mean(x * x, axis=-1, keepdims=True) + RMS_EPS)
    return (y * gain) * (1.0 + scale) + shift


def _mod_kernel(c_ref, w_ref, b_ref, o_ref):
    c = c_ref[...]
    a = (c * jax.nn.sigmoid(c)).astype(BF16)
    o_ref[...] = jnp.dot(a, w_ref[...].astype(BF16), preferred_element_type=F32) + b_ref[...]


def _modulation(c, w_mod, b_mod, *, tn=512):
    bp, d = c.shape
    n = w_mod.shape[1]
    return pl.pallas_call(
        _mod_kernel,
        out_shape=jax.ShapeDtypeStruct((bp, n), F32),
        grid=(n // tn,),
        in_specs=[pl.BlockSpec((bp, d), lambda j: (0, 0)),
                  pl.BlockSpec((d, tn), lambda j: (0, j)),
                  pl.BlockSpec((1, tn), lambda j: (0, j))],
        out_specs=pl.BlockSpec((bp, tn), lambda j: (0, j)),
        name="modulation",
        compiler_params=_params("arbitrary"),
    )(c, w_mod, b_mod.reshape(1, n))


def _proj_kernel(x0_ref, xn_ref, mod0_ref, modn_ref, g_ref, w_ref, b_ref, o_ref,
                 h_even_ref, h_odd_ref, *, n_in_tiles, n_col_tiles, row_chunk):
    i = pl.program_id(0)
    j = pl.program_id(1)
    odd = lax.rem(i, 2) == 1
    gain = g_ref[...]
    tm = xn_ref.shape[0]
    rows_per_step = tm // n_col_tiles

    def norm_rows(x_ref, mod_ref, dst_ref, row0, n_rows):
        rows = pl.ds(pl.multiple_of(row0, n_rows), n_rows)
        h = _rms_modulate(x_ref[rows, :], gain, mod_ref[0, 1:2, :], mod_ref[0, 0:1, :])
        dst_ref[rows, :] = h.astype(BF16)

    @pl.when((i == 0) & (j == 0))
    def _():
        def body(r, carry):
            norm_rows(x0_ref, mod0_ref, h_even_ref, r * row_chunk, row_chunk)
            return carry

        lax.fori_loop(0, tm // row_chunk, body, 0)

    def column_step(cur_ref, nxt_ref, gated):
        norm_rows(xn_ref, modn_ref, nxt_ref, j * rows_per_step, rows_per_step)
        acc = jnp.dot(cur_ref[...], w_ref[...], preferred_element_type=F32)
        if gated:
            acc = 0.5 * jnp.tanh(0.5 * (acc + b_ref[...])) + 0.5
        o_ref[...] = acc.astype(BF16)

    for is_odd, cur_ref, nxt_ref in ((False, h_even_ref, h_odd_ref),
                                     (True, h_odd_ref, h_even_ref)):
        for gated in (False, True):
            pl.when((odd == is_odd) & ((j >= n_in_tiles) == gated))(
                functools.partial(column_step, cur_ref, nxt_ref, gated))


def _projection(x2d, mod, b_off, seq, g_mix, w_pair, b_gate, *, tm=1024, tn=1024):
    m, d = x2d.shape
    n = 2 * w_pair.shape[2]
    n_in_tiles = w_pair.shape[2] // tn
    n_col_tiles = n // tn
    tiles_per_seq = seq // tm
    last = m // tm - 1
    kern = functools.partial(_proj_kernel, n_in_tiles=n_in_tiles, n_col_tiles=n_col_tiles,
                             row_chunk=256)
    nxt = lambda i: jnp.minimum(i + 1, last)
    return pl.pallas_call(
        kern,
        out_shape=jax.ShapeDtypeStruct((m, n), BF16),
        grid=(m // tm, n_col_tiles),
        in_specs=[_resident((tm, d), lambda i, j: (0, 0)),
                  pl.BlockSpec((tm, d), lambda i, j: (nxt(i), 0)),
                  _resident((1, N_MOD, d), lambda i, j: (b_off, 0, 0)),
                  pl.BlockSpec((1, N_MOD, d),
                               lambda i, j: (b_off + nxt(i) // tiles_per_seq, 0, 0)),
                  pl.BlockSpec((1, d), lambda i, j: (0, 0)),
                  pl.BlockSpec((None, d, tn),
                               lambda i, j: (j // n_in_tiles, 0, lax.rem(j, n_in_tiles))),
                  pl.BlockSpec((1, tn), lambda i, j: (0, jnp.maximum(j - n_in_tiles, 0)))],
        out_specs=pl.BlockSpec((tm, tn), lambda i, j: (i, j)),
        scratch_shapes=[pltpu.VMEM((tm, d), BF16), pltpu.VMEM((tm, d), BF16)],
        name="projection",
        compiler_params=_params("arbitrary", "arbitrary"),
    )(x2d, x2d, mod, mod, g_mix.reshape(1, d), w_pair, b_gate.reshape(1, -1))


def _attn_kernel(slopes_ref, sink_ref, q_ref, k_ref, v_ref, o_ref,
                 bias_ref, v1_ref, s0_ref, s1_ref, p0_ref, p1_ref, e0_ref, e1_ref, *, seq):
    s_refs, p_refs, e_refs = (s0_ref, s1_ref), (p0_ref, p1_ref), (e0_ref, e1_ref)
    kh = pl.program_id(1)
    n_chunks = 3
    span = n_chunks * BLOCK
    n_blocks = seq // BLOCK
    logit_scale = (HEAD_DIM ** -0.5) * LOG2E

    q_pos = lax.broadcasted_iota(jnp.int32, (BLOCK, BLOCK), 0)
    k_pos = lax.broadcasted_iota(jnp.int32, (BLOCK, BLOCK), 1)
    for d in range(-2, 3):
        rel = d * BLOCK + k_pos - q_pos
        valid = jnp.abs(rel) <= WINDOW
        dist = jnp.abs(rel).astype(F32)
        for g in range(Q_PER_KV):
            slope2 = slopes_ref[kh * Q_PER_KV + g] * LOG2E
            bias_ref[d + 2, g * BLOCK:(g + 1) * BLOCK, :] = jnp.where(valid, -slope2 * dist,
                                                                        -jnp.inf)
    sinks2 = [sink_ref[kh * Q_PER_KV + g] * LOG2E for g in range(Q_PER_KV)]
    v1_ref[:, :HEAD_DIM] = v_ref[...]
    v1_ref[:, HEAD_DIM:] = jnp.ones((seq, HEAD_DIM), BF16)

    def window(i):
        q0 = pl.multiple_of(i * BLOCK, BLOCK)
        k0 = pl.multiple_of(jnp.clip(q0 - BLOCK, 0, seq - span), BLOCK)
        return q0, k0

    def logits(i, s_ref):
        q0, k0 = window(i)
        first = lax.div(k0 - q0 + 2 * BLOCK, BLOCK)
        q = q_ref[pl.ds(q0, BLOCK), :]
        qs = jnp.concatenate([q[:, g * HEAD_DIM:(g + 1) * HEAD_DIM] for g in range(Q_PER_KV)],
                             axis=0)
        kw = k_ref[pl.ds(k0, span), :]
        s = lax.dot_general(qs, kw, (((1,), (1,)), ((), ())), preferred_element_type=F32)
        for t in range(n_chunks):
            cols = slice(t * BLOCK, (t + 1) * BLOCK)
            s_ref[:, cols] = s[:, cols] * logit_scale + bias_ref[first + t]

    def softmax(s_ref, p_ref, e_ref):
        for g in range(Q_PER_KV):
            rows = slice(g * BLOCK, (g + 1) * BLOCK)
            a = s_ref[rows, :]
            mx = jnp.maximum(jnp.max(a, axis=-1, keepdims=True), sinks2[g])
            p_ref[rows, :] = jnp.exp2(a - mx).astype(BF16)
            e_ref[rows, :] = jnp.broadcast_to(jnp.exp2(sinks2[g] - mx), (BLOCK, HEAD_DIM))

    def values(i, p_ref, e_ref):
        q0, k0 = window(i)
        r = jnp.dot(p_ref[...], v1_ref[pl.ds(k0, span), :], preferred_element_type=F32)
        out = (r[:, :HEAD_DIM] / (r[:, HEAD_DIM:] + e_ref[...])).astype(BF16)
        for g in range(Q_PER_KV):
            o_ref[pl.ds(q0, BLOCK), g * HEAD_DIM:(g + 1) * HEAD_DIM] = (
                out[g * BLOCK:(g + 1) * BLOCK])

    def pair(i, n_logits, n_softmax):
        for half in range(2):
            blk = i + half
            if half < n_logits:
                logits(blk + 2, s_refs[half])
            if half < n_softmax:
                softmax(s_refs[1 - half], p_refs[1 - half], e_refs[1 - half])
            values(blk, p_refs[half], e_refs[half])

    logits(0, s_refs[0])
    logits(1, s_refs[1])
    softmax(s_refs[0], p_refs[0], e_refs[0])

    def body(it, carry):
        pair(2 * it, 2, 2)
        return carry

    lax.fori_loop(0, n_blocks // 2 - 1, body, 0)
    pair(n_blocks - 2, 0, 1)


def _attention(proj, slopes, sink, batch, seq, *, attn_width, kv_width):
    m = proj.shape[0]
    n_kv = kv_width // HEAD_DIM
    gw = Q_PER_KV * HEAD_DIM
    k_col0 = attn_width // HEAD_DIM
    v_col0 = (attn_width + kv_width) // HEAD_DIM
    smem = pl.BlockSpec(memory_space=pltpu.SMEM)
    rows = Q_PER_KV * BLOCK
    return pl.pallas_call(
        functools.partial(_attn_kernel, seq=seq),
        out_shape=jax.ShapeDtypeStruct((m, attn_width), BF16),
        grid=(batch, n_kv),
        in_specs=[smem, smem,
                  pl.BlockSpec((seq, gw), lambda b, h: (b, h)),
                  pl.BlockSpec((seq, HEAD_DIM), lambda b, h: (b, k_col0 + h)),
                  pl.BlockSpec((seq, HEAD_DIM), lambda b, h: (b, v_col0 + h))],
        out_specs=pl.BlockSpec((seq, gw), lambda b, h: (b, h)),
        scratch_shapes=[pltpu.VMEM((5, rows, BLOCK), F32),
                        pltpu.VMEM((seq, 2 * HEAD_DIM), BF16),
                        *[pltpu.VMEM((rows, 3 * BLOCK), F32) for _ in range(2)],
                        *[pltpu.VMEM((rows, 3 * BLOCK), BF16) for _ in range(2)],
                        *[pltpu.VMEM((rows, HEAD_DIM), F32) for _ in range(2)]],
        name="attention",
        compiler_params=_params("parallel", "arbitrary"),
    )(slopes, sink, proj, proj, proj)


def _fourier_kernel(u_ref, cd_ref, sd_ref, ch_ref, sh_ref, pm_ref, o_ref,
                    ue_ref, uo_ref, ae_ref, bo_ref, y_ref, *, seq, gd):
    half = seq // 2
    n_blocks = half // BLOCK
    pm = pm_ref[...]

    def reversed_block(src_ref, b, end):
        lo = end - (b + 1) * BLOCK
        if b == 0:
            return jnp.dot(pm[:, :BLOCK], src_ref[lo:end, :], preferred_element_type=F32)
        return jnp.dot(pm, src_ref[lo:lo + 2 * BLOCK, :], preferred_element_type=F32)

    for b in range(n_blocks):
        rows = slice(b * BLOCK, (b + 1) * BLOCK)
        x = u_ref[rows, :].astype(F32)
        r = reversed_block(u_ref, b, seq)
        ue_ref[rows, :] = (x + r).astype(BF16)
        uo_ref[rows, :] = (x - r).astype(BF16)

    mid = []
    for g in range(N_FOURIER_GROUPS):
        cols = slice(g * gd, (g + 1) * gd)
        ae_ref[:, cols] = jnp.dot(ue_ref[:, cols], cd_ref[...],
                                  preferred_element_type=F32).astype(BF16)
        bo_ref[:, cols] = jnp.dot(uo_ref[:, cols], sd_ref[...],
                                  preferred_element_type=F32).astype(BF16)
        mid.append(jnp.dot(u_ref[half:half + BF16_SUBLANE_TILE, cols], cd_ref[...],
                           preferred_element_type=F32)[0:1])
    a_mid = jnp.concatenate(mid, axis=1) * (seq ** -0.5)

    p = jnp.dot(ch_ref[...], ae_ref[...], preferred_element_type=F32)
    q = jnp.dot(sh_ref[...], bo_ref[...], preferred_element_type=F32)
    k_idx = lax.broadcasted_iota(jnp.int32, (half, 1), 0)
    sign = 1.0 - 2.0 * (k_idx & 1).astype(F32)
    base = p[:half] + sign * a_mid
    o_ref[0:half, :] = (base - q).astype(BF16)
    y_ref[...] = (base + q).astype(BF16)
    z_mid = p[half:half + 1] + a_mid

    for b in range(n_blocks):
        r = reversed_block(y_ref, b, half)
        if b == 0:
            first_row = lax.broadcasted_iota(jnp.int32, (BLOCK, 1), 0) == 0
            r = jnp.where(first_row, z_mid, r)
        o_ref[half + b * BLOCK:half + (b + 1) * BLOCK, :] = r.astype(BF16)


def _dft_tables(seq, gd):
    half = seq // 2

    def angles(rows, cols, n):
        k = jnp.arange(rows, dtype=jnp.int32)[:, None]
        s = jnp.arange(cols, dtype=jnp.int32)[None, :]
        return ((k * s) % n).astype(F32) * (2.0 * jnp.pi / n)

    ang_c = angles(gd, gd, gd)
    cd = jnp.cos(ang_c) * (gd ** -0.5)
    sd = jnp.sin(ang_c) * (gd ** -0.5)
    pad_rows = half + BF16_SUBLANE_TILE
    fine = 32
    k = jnp.arange(pad_rows, dtype=jnp.int32)[:, None]
    s_hi = fine * jnp.arange(half // fine, dtype=jnp.int32)[None, :]
    ang_hi = (((k * s_hi) % seq).astype(F32) * (2.0 * jnp.pi / seq))[:, :, None]
    ang_lo = angles(pad_rows, fine, seq)[:, None, :]
    cos_p = (jnp.cos(ang_hi) * jnp.cos(ang_lo) - jnp.sin(ang_hi) * jnp.sin(ang_lo))
    sin_p = (jnp.sin(ang_hi) * jnp.cos(ang_lo) + jnp.cos(ang_hi) * jnp.sin(ang_lo))
    live = jnp.arange(pad_rows)[:, None] <= half
    ch = jnp.where(live, cos_p.reshape(pad_rows, half), 0.0) * (seq ** -0.5)
    sh = sin_p.reshape(pad_rows, half)[:half] * (seq ** -0.5)
    a = jnp.arange(BLOCK)[:, None]
    j = jnp.arange(2 * BLOCK)[None, :]
    pm = jnp.where(a == 0, j == BLOCK, j == BLOCK - a)
    return tuple(t.astype(BF16) for t in (cd, sd, ch, sh, pm))


def _fourier(proj, tables, batch, seq, *, u_col0, f_width):
    m = proj.shape[0]
    gd = f_width // N_FOURIER_GROUPS
    half = seq // 2
    cd, sd, ch, sh, pm = tables
    const = lambda shape: _resident(shape, lambda b: (0, 0))
    return pl.pallas_call(
        functools.partial(_fourier_kernel, seq=seq, gd=gd),
        out_shape=jax.ShapeDtypeStruct((m, f_width), BF16),
        grid=(batch,),
        in_specs=[pl.BlockSpec((seq, f_width), lambda b: (b, u_col0 // f_width)),
                  const(cd.shape), const(sd.shape), const(ch.shape), const(sh.shape),
                  const(pm.shape)],
        out_specs=pl.BlockSpec((seq, f_width), lambda b: (b, 0)),
        scratch_shapes=[pltpu.VMEM((half, f_width), BF16) for _ in range(5)],
        name="fourier",
        compiler_params=_params("parallel"),
    )(proj, cd, sd, ch, sh, pm)


def _merge_kernel(a_ref, f_ref, ga_ref, gf_ref, wa_ref, wf_ref, o_ref):
    ta = jnp.dot(a_ref[...], wa_ref[...], preferred_element_type=F32)
    tf = jnp.dot(f_ref[...], wf_ref[...], preferred_element_type=F32)
    o_ref[...] = (ga_ref[...].astype(F32) * ta + gf_ref[...].astype(F32) * tf).astype(BF16)


def _merge(attn, four, projg, wa, wf, *, gate_col0, tm=1024, tn=1024):
    m, aw = attn.shape
    fw = four.shape[1]
    d = wa.shape[1]
    ga0 = gate_col0 // tn
    gf0 = (gate_col0 + d) // tn
    return pl.pallas_call(
        _merge_kernel,
        out_shape=jax.ShapeDtypeStruct((m, d), BF16),
        grid=(m // tm, d // tn),
        in_specs=[pl.BlockSpec((tm, aw), lambda i, j: (i, 0)),
                  pl.BlockSpec((tm, fw), lambda i, j: (i, 0)),
                  pl.BlockSpec((tm, tn), lambda i, j: (i, ga0 + j)),
                  pl.BlockSpec((tm, tn), lambda i, j: (i, gf0 + j)),
                  pl.BlockSpec((aw, tn), lambda i, j: (0, j)),
                  pl.BlockSpec((fw, tn), lambda i, j: (0, j))],
        out_specs=pl.BlockSpec((tm, tn), lambda i, j: (i, j)),
        name="merge",
        compiler_params=_params("parallel", "arbitrary"),
    )(attn, four, projg, projg, wa, wf)


def _out_kernel(m_ref, x_ref, mod_ref, g_ref, wo_ref, x1_ref, h2_ref):
    out = jnp.dot(m_ref[...], wo_ref[...], preferred_element_type=F32)
    x1 = x_ref[...] + mod_ref[0, 2:3, :] * out
    x1_ref[...] = x1
    h2 = _rms_modulate(x1, g_ref[...], mod_ref[0, 4:5, :], mod_ref[0, 3:4, :])
    h2_ref[...] = h2.astype(BF16)


def _out_proj(merged, x2d, mod, b_off, seq, g_ffn, wo, *, tm=512):
    m, d = x2d.shape
    tiles_per_seq = seq // tm
    return pl.pallas_call(
        _out_kernel,
        out_shape=(jax.ShapeDtypeStruct((m, d), F32), jax.ShapeDtypeStruct((m, d), BF16)),
        grid=(m // tm,),
        in_specs=[pl.BlockSpec((tm, d), lambda i: (i, 0)),
                  pl.BlockSpec((tm, d), lambda i: (i, 0)),
                  pl.BlockSpec((1, N_MOD, d), lambda i: (b_off + i // tiles_per_seq, 0, 0)),
                  pl.BlockSpec((1, d), lambda i: (0, 0)),
                  _resident((d, d), lambda i: (0, 0))],
        out_specs=(pl.BlockSpec((tm, d), lambda i: (i, 0)),
                   pl.BlockSpec((tm, d), lambda i: (i, 0))),
        name="out_proj",
        compiler_params=_params("parallel"),
    )(merged, x2d, mod, g_ffn.reshape(1, d), wo)


def _ffn_up_kernel(h_ref, wg_ref, wu_ref, o_ref):
    h = h_ref[...]
    gt = jnp.dot(h, wg_ref[...], preferred_element_type=F32)
    up = jnp.dot(h, wu_ref[...], preferred_element_type=F32)
    o_ref[...] = (gt * jax.nn.sigmoid(gt) * up).astype(BF16)


def _ffn_up(h2, w_up, *, tm=512, tn=2816):
    m, d = h2.shape
    dff = w_up.shape[1] // 2
    nj = dff // tn
    return pl.pallas_call(
        _ffn_up_kernel,
        out_shape=jax.ShapeDtypeStruct((m, dff), BF16),
        grid=(nj, m // tm),
        in_specs=[pl.BlockSpec((tm, d), lambda j, i: (i, 0)),
                  _resident((d, tn), lambda j, i: (0, j)),
                  _resident((d, tn), lambda j, i: (0, nj + j))],
        out_specs=pl.BlockSpec((tm, tn), lambda j, i: (i, j)),
        name="ffn_up",
        compiler_params=_params("arbitrary", "parallel"),
    )(h2, w_up, w_up)


def _ffn_down_kernel(a_ref, x1_ref, mod_ref, g_ref, wd_ref, y_ref, *, final_norm):
    out = jnp.dot(a_ref[...], wd_ref[...], preferred_element_type=F32)
    x2 = x1_ref[...] + mod_ref[0, 5:6, :] * out
    if final_norm:
        y = x2 * lax.rsqrt(jnp.mean(x2 * x2, axis=-1, keepdims=True) + RMS_EPS)
        x2 = y * g_ref[...]
    y_ref[...] = x2


def _ffn_down(act, x1, mod, b_off, seq, g_final, wd, *, final_norm, tm=256):
    m, d = x1.shape
    dff = act.shape[1]
    tiles_per_seq = seq // tm
    return pl.pallas_call(
        functools.partial(_ffn_down_kernel, final_norm=final_norm),
        out_shape=jax.ShapeDtypeStruct((m, d), F32),
        grid=(m // tm,),
        in_specs=[pl.BlockSpec((tm, dff), lambda i: (i, 0)),
                  pl.BlockSpec((tm, d), lambda i: (i, 0)),
                  pl.BlockSpec((1, N_MOD, d), lambda i: (b_off + i // tiles_per_seq, 0, 0)),
                  pl.BlockSpec((1, d), lambda i: (0, 0)),
                  _resident((dff, d), lambda i: (0, 0))],
        out_specs=pl.BlockSpec((tm, d), lambda i: (i, 0)),
        name="ffn_down",
        compiler_params=_params("parallel"),
    )(act, x1, mod, g_final.reshape(1, d), wd)


def kernel(x_prompt, x_sample, c_prompt, c_sample, w_mod, b_mod, g_mix, w_in, attn_sink,
           w_attn_branch, w_fourier_branch, w_gate, b_gate, w_out, g_ffn, w_up, w_down,
           g_final):
    depth, d, in_width = w_in.shape
    attn_width = w_attn_branch.shape[1]
    f_width = w_fourier_branch.shape[1]
    kv_width = (in_width - attn_width - f_width) // 2
    n_heads = attn_width // HEAD_DIM
    groups = [(x_prompt, 0), (x_sample, c_prompt.shape[0])]
    seq = x_prompt.shape[1]
    assert x_sample.shape[1] == seq and seq % (4 * BLOCK) == 0
    assert (attn_width + 2 * kv_width) % f_width == 0
    assert w_gate.shape[2] == in_width

    c_all = jnp.concatenate([c_prompt, c_sample], axis=0)
    pad = -c_all.shape[0] % BF16_SUBLANE_TILE
    c_all = jnp.pad(c_all, ((0, pad), (0, 0)))

    heads = jnp.arange(1, n_heads + 1, dtype=F32)
    slopes = jnp.exp2(-8.0 * heads / n_heads)
    dft_tables = _dft_tables(seq, f_width // N_FOURIER_GROUPS)

    xs = [x.reshape(-1, d) for x, _ in groups]
    for l in range(depth):
        last = l == depth - 1
        mod = _modulation(c_all, w_mod[l], b_mod[l]).reshape(-1, N_MOD, d)
        w_pair = jnp.stack([w_in[l], w_gate[l]]).astype(BF16)
        wa = w_attn_branch[l].astype(BF16)
        wf = w_fourier_branch[l].astype(BF16)
        wo = w_out[l].astype(BF16)
        wu = w_up[l].astype(BF16)
        wd = w_down[l].astype(BF16)
        sink = attn_sink[l].astype(F32)
        new_xs = []
        for x2d, (x_in, b_off) in zip(xs, groups):
            batch = x_in.shape[0]
            projg = _projection(x2d, mod, b_off, seq, g_mix[l], w_pair, b_gate[l])
            attn = _attention(projg, slopes, sink, batch, seq,
                              attn_width=attn_width, kv_width=kv_width)
            four = _fourier(projg, dft_tables, batch, seq,
                            u_col0=attn_width + 2 * kv_width, f_width=f_width)
            merged = _merge(attn, four, projg, wa, wf, gate_col0=in_width)
            x1, h2 = _out_proj(merged, x2d, mod, b_off, seq, g_ffn[l], wo)
            act = _ffn_up(h2, wu)
            new_xs.append(_ffn_down(act, x1, mod, b_off, seq, g_final, wd, final_norm=last))
        xs = new_xs
    return tuple(x2d.reshape(x_in.shape) for x2d, (x_in, _) in zip(xs, groups))
```

```python
import functools
import math

import jax
import jax.numpy as jnp
from jax import lax
from jax.experimental import pallas as pl
from jax.experimental.pallas import tpu as pltpu

F32 = jnp.float32
BF16 = jnp.bfloat16

HEAD_DIM = 128
Q_PER_KV = 4
N_FOURIER_GROUPS = 4
WINDOW = 128
BLOCK = 128
N_MOD = 6
RMS_EPS = 1e-6
LOG2E = math.log2(math.e)

V7X_VMEM_LIMIT_BYTES = 60 * 1024 * 1024
BF16_SUBLANE_TILE = 16


def _params(*semantics):
    return pltpu.CompilerParams(dimension_semantics=semantics,
                                vmem_limit_bytes=V7X_VMEM_LIMIT_BYTES)


def _resident(block_shape, index_map):
    return pl.BlockSpec(block_shape, index_map, pipeline_mode=pl.Buffered(1))


def _rms_modulate(x, gain, scale, shift):
    y = x * lax.rsqrt(jnp.mean(x * x, axis=-1, keepdims=True) + RMS_EPS)
    return y * (gain * (1.0 + scale)) + shift


def _mod_kernel(c_ref, w_ref, b_ref, o_ref):
    c = c_ref[...]
    a = (c * jax.nn.sigmoid(c)).astype(BF16)
    o_ref[...] = jnp.dot(a, w_ref[...].astype(BF16), preferred_element_type=F32) + b_ref[...]


def _modulation(c, w_mod, b_mod, *, tn=512):
    bp, d = c.shape
    n = w_mod.shape[1]
    return pl.pallas_call(
        _mod_kernel,
        out_shape=jax.ShapeDtypeStruct((bp, n), F32),
        grid=(n // tn,),
        in_specs=[pl.BlockSpec((bp, d), lambda j: (0, 0)),
                  pl.BlockSpec((d, tn), lambda j: (0, j)),
                  pl.BlockSpec((1, tn), lambda j: (0, j))],
        out_specs=pl.BlockSpec((bp, tn), lambda j: (0, j)),
        name="modulation",
        compiler_params=_params("arbitrary"),
    )(c, w_mod, b_mod.reshape(1, n))


def _proj_kernel(x0_ref, xn_ref, mod0_ref, modn_ref, g_ref, w_ref, b_ref, o_ref,
                 h_even_ref, h_odd_ref, *, n_in_tiles, n_col_tiles, row_chunk):
    i = pl.program_id(0)
    j = pl.program_id(1)
    odd = lax.rem(i, 2) == 1
    gain = g_ref[...]
    tm = xn_ref.shape[0]
    rows_per_step = tm // n_col_tiles

    def norm_rows(x_ref, mod_ref, dst_ref, row0, n_rows):
        rows = pl.ds(pl.multiple_of(row0, n_rows), n_rows)
        h = _rms_modulate(x_ref[rows, :], gain, mod_ref[0, 1:2, :], mod_ref[0, 0:1, :])
        dst_ref[rows, :] = h.astype(BF16)

    @pl.when((i == 0) & (j == 0))
    def _():
        def body(r, carry):
            norm_rows(x0_ref, mod0_ref, h_even_ref, r * row_chunk, row_chunk)
            return carry

        lax.fori_loop(0, tm // row_chunk, body, 0)

    def column_step(cur_ref, nxt_ref, gated):
        norm_rows(xn_ref, modn_ref, nxt_ref, j * rows_per_step, rows_per_step)
        acc = jnp.dot(cur_ref[...], w_ref[...], preferred_element_type=F32)
        if gated:
            acc = 0.5 * jnp.tanh(0.5 * (acc + b_ref[...])) + 0.5
        o_ref[...] = acc.astype(BF16)

    for is_odd, cur_ref, nxt_ref in ((False, h_even_ref, h_odd_ref),
                                     (True, h_odd_ref, h_even_ref)):
        for gated in (False, True):
            pl.when((odd == is_odd) & ((j >= n_in_tiles) == gated))(
                functools.partial(column_step, cur_ref, nxt_ref, gated))


def _projection(x2d, mod, b_off, seq, g_mix, w_pair, b_gate, *, tm=1024, tn=1024):
    m, d = x2d.shape
    n = 2 * w_pair.shape[2]
    n_in_tiles = w_pair.shape[2] // tn
    n_col_tiles = n // tn
    tiles_per_seq = seq // tm
    last = m // tm - 1
    kern = functools.partial(_proj_kernel, n_in_tiles=n_in_tiles, n_col_tiles=n_col_tiles,
                             row_chunk=256)
    nxt = lambda i: jnp.minimum(i + 1, last)
    return pl.pallas_call(
        kern,
        out_shape=jax.ShapeDtypeStruct((m, n), BF16),
        grid=(m // tm, n_col_tiles),
        in_specs=[_resident((tm, d), lambda i, j: (0, 0)),
                  pl.BlockSpec((tm, d), lambda i, j: (nxt(i), 0)),
                  _resident((1, N_MOD, d), lambda i, j: (b_off, 0, 0)),
                  pl.BlockSpec((1, N_MOD, d),
                               lambda i, j: (b_off + nxt(i) // tiles_per_seq, 0, 0)),
                  pl.BlockSpec((1, d), lambda i, j: (0, 0)),
                  pl.BlockSpec((None, d, tn),
                               lambda i, j: (j // n_in_tiles, 0, lax.rem(j, n_in_tiles))),
                  pl.BlockSpec((1, tn), lambda i, j: (0, jnp.maximum(j - n_in_tiles, 0)))],
        out_specs=pl.BlockSpec((tm, tn), lambda i, j: (i, j)),
        scratch_shapes=[pltpu.VMEM((tm, d), BF16), pltpu.VMEM((tm, d), BF16)],
        name="projection",
        compiler_params=_params("arbitrary", "arbitrary"),
    )(x2d, x2d, mod, mod, g_mix.reshape(1, d), w_pair, b_gate.reshape(1, -1))


def _attn_kernel(slopes_ref, sink_ref, q_ref, k_ref, v_ref, o_ref,
                 bias_ref, v1_ref, s0_ref, s1_ref, p0_ref, p1_ref, e0_ref, e1_ref, *, seq):
    s_refs, p_refs, e_refs = (s0_ref, s1_ref), (p0_ref, p1_ref), (e0_ref, e1_ref)
    kh = pl.program_id(1)
    n_chunks = 3
    span = n_chunks * BLOCK
    n_blocks = seq // BLOCK
    logit_scale = (HEAD_DIM ** -0.5) * LOG2E

    @pl.when(pl.program_id(0) == 0)
    def _():
        q_pos = lax.broadcasted_iota(jnp.int32, (BLOCK, BLOCK), 0)
        k_pos = lax.broadcasted_iota(jnp.int32, (BLOCK, BLOCK), 1)
        for d in range(-2, 3):
            rel = d * BLOCK + k_pos - q_pos
            valid = jnp.abs(rel) <= WINDOW
            dist = jnp.abs(rel).astype(F32)
            for g in range(Q_PER_KV):
                slope2 = slopes_ref[kh * Q_PER_KV + g] * LOG2E
                bias_ref[kh, d + 2, g * BLOCK:(g + 1) * BLOCK, :] = jnp.where(
                    valid, -slope2 * dist, -jnp.inf)
        v1_ref[:, HEAD_DIM:] = jnp.ones((seq, HEAD_DIM), BF16)

    sinks2 = [sink_ref[kh * Q_PER_KV + g] * LOG2E for g in range(Q_PER_KV)]
    v1_ref[:, :HEAD_DIM] = v_ref[...]

    def window(i):
        q0 = pl.multiple_of(i * BLOCK, BLOCK)
        k0 = pl.multiple_of(jnp.clip(q0 - BLOCK, 0, seq - span), BLOCK)
        return q0, k0

    def logits(i, s_ref):
        q0, k0 = window(i)
        first = lax.div(k0 - q0 + 2 * BLOCK, BLOCK)
        q = q_ref[pl.ds(q0, BLOCK), :]
        qs = jnp.concatenate([q[:, g * HEAD_DIM:(g + 1) * HEAD_DIM] for g in range(Q_PER_KV)],
                             axis=0)
        kw = k_ref[pl.ds(k0, span), :]
        s = lax.dot_general(qs, kw, (((1,), (1,)), ((), ())), preferred_element_type=F32)
        for t in range(n_chunks):
            cols = slice(t * BLOCK, (t + 1) * BLOCK)
            s_ref[:, cols] = s[:, cols] * logit_scale + bias_ref[kh, first + t]

    def softmax(s_ref, p_ref, e_ref):
        for g in range(Q_PER_KV):
            rows = slice(g * BLOCK, (g + 1) * BLOCK)
            a = s_ref[rows, :]
            mx = jnp.maximum(jnp.max(a, axis=-1, keepdims=True), sinks2[g])
            p_ref[rows, :] = jnp.exp2(a - mx).astype(BF16)
            e_ref[rows, :] = jnp.broadcast_to(jnp.exp2(sinks2[g] - mx), (BLOCK, HEAD_DIM))

    def values(i, p_ref, e_ref):
        q0, k0 = window(i)
        r = jnp.dot(p_ref[...], v1_ref[pl.ds(k0, span), :], preferred_element_type=F32)
        out = (r[:, :HEAD_DIM] / (r[:, HEAD_DIM:] + e_ref[...])).astype(BF16)
        for g in range(Q_PER_KV):
            o_ref[pl.ds(q0, BLOCK), g * HEAD_DIM:(g + 1) * HEAD_DIM] = (
                out[g * BLOCK:(g + 1) * BLOCK])

    def pair(i, n_logits, n_softmax):
        for half in range(2):
            blk = i + half
            if half < n_logits:
                logits(blk + 2, s_refs[half])
            if half < n_softmax:
                softmax(s_refs[1 - half], p_refs[1 - half], e_refs[1 - half])
            values(blk, p_refs[half], e_refs[half])

    logits(0, s_refs[0])
    logits(1, s_refs[1])
    softmax(s_refs[0], p_refs[0], e_refs[0])

    def body(it, carry):
        pair(4 * it, 2, 2)
        pair(4 * it + 2, 2, 2)
        return carry

    full_pairs = n_blocks // 2 - 1
    lax.fori_loop(0, full_pairs // 2, body, 0)
    if full_pairs % 2:
        pair(n_blocks - 4, 2, 2)
    pair(n_blocks - 2, 0, 1)


def _attention(proj, slopes, sink, batch, seq, *, attn_width, kv_width):
    m = proj.shape[0]
    n_kv = kv_width // HEAD_DIM
    gw = Q_PER_KV * HEAD_DIM
    k_col0 = attn_width // HEAD_DIM
    v_col0 = (attn_width + kv_width) // HEAD_DIM
    smem = pl.BlockSpec(memory_space=pltpu.SMEM)
    rows = Q_PER_KV * BLOCK
    return pl.pallas_call(
        functools.partial(_attn_kernel, seq=seq),
        out_shape=jax.ShapeDtypeStruct((m, attn_width), BF16),
        grid=(batch, n_kv),
        in_specs=[smem, smem,
                  pl.BlockSpec((seq, gw), lambda b, h: (b, h)),
                  pl.BlockSpec((seq, HEAD_DIM), lambda b, h: (b, k_col0 + h)),
                  pl.BlockSpec((seq, HEAD_DIM), lambda b, h: (b, v_col0 + h))],
        out_specs=pl.BlockSpec((seq, gw), lambda b, h: (b, h)),
        scratch_shapes=[pltpu.VMEM((n_kv, 5, rows, BLOCK), F32),
                        pltpu.VMEM((seq, 2 * HEAD_DIM), BF16),
                        *[pltpu.VMEM((rows, 3 * BLOCK), F32) for _ in range(2)],
                        *[pltpu.VMEM((rows, 3 * BLOCK), BF16) for _ in range(2)],
                        *[pltpu.VMEM((rows, HEAD_DIM), F32) for _ in range(2)]],
        name="attention",
        compiler_params=_params("arbitrary", "arbitrary"),
    )(slopes, sink, proj, proj, proj)


def _fourier_kernel(u_ref, cd_ref, sd_ref, ch_ref, sh_ref, pm_ref, o_ref,
                    ue_ref, uo_ref, ae_ref, bo_ref, y_ref, *, seq, gd):
    half = seq // 2
    n_blocks = half // BLOCK
    pm = pm_ref[...]

    def reversed_block(src_ref, b, end):
        lo = end - (b + 1) * BLOCK
        if b == 0:
            return jnp.dot(pm[:, :BLOCK], src_ref[lo:end, :], preferred_element_type=F32)
        return jnp.dot(pm, src_ref[lo:lo + 2 * BLOCK, :], preferred_element_type=F32)

    for b in range(n_blocks):
        rows = slice(b * BLOCK, (b + 1) * BLOCK)
        x = u_ref[rows, :].astype(F32)
        r = reversed_block(u_ref, b, seq)
        ue_ref[rows, :] = (x + r).astype(BF16)
        uo_ref[rows, :] = (x - r).astype(BF16)

    mid = []
    for g in range(N_FOURIER_GROUPS):
        cols = slice(g * gd, (g + 1) * gd)
        ae_ref[:, cols] = jnp.dot(ue_ref[:, cols], cd_ref[...],
                                  preferred_element_type=F32).astype(BF16)
        bo_ref[:, cols] = jnp.dot(uo_ref[:, cols], sd_ref[...],
                                  preferred_element_type=F32).astype(BF16)
        mid.append(jnp.dot(u_ref[half:half + BF16_SUBLANE_TILE, cols], cd_ref[...],
                           preferred_element_type=F32)[0:1])
    a_mid = jnp.concatenate(mid, axis=1) * (seq ** -0.5)

    p = jnp.dot(ch_ref[...], ae_ref[...], preferred_element_type=F32)
    q = jnp.dot(sh_ref[...], bo_ref[...], preferred_element_type=F32)
    k_idx = lax.broadcasted_iota(jnp.int32, (half, 1), 0)
    sign = 1.0 - 2.0 * (k_idx & 1).astype(F32)
    base = p[:half] + sign * a_mid
    o_ref[0:half, :] = (base - q).astype(BF16)
    y_ref[...] = (base + q).astype(BF16)
    z_mid = p[half:half + 1] + a_mid

    for b in range(n_blocks):
        r = reversed_block(y_ref, b, half)
        if b == 0:
            first_row = lax.broadcasted_iota(jnp.int32, (BLOCK, 1), 0) == 0
            r = jnp.where(first_row, z_mid, r)
        o_ref[half + b * BLOCK:half + (b + 1) * BLOCK, :] = r.astype(BF16)


def _dft_tables(seq, gd):
    half = seq // 2

    def angles(rows, cols, n):
        k = jnp.arange(rows, dtype=jnp.int32)[:, None]
        s = jnp.arange(cols, dtype=jnp.int32)[None, :]
        return ((k * s) % n).astype(F32) * (2.0 * jnp.pi / n)

    ang_c = angles(gd, gd, gd)
    cd = jnp.cos(ang_c) * (gd ** -0.5)
    sd = jnp.sin(ang_c) * (gd ** -0.5)
    pad_rows = half + BF16_SUBLANE_TILE
    fine = 32
    k = jnp.arange(pad_rows, dtype=jnp.int32)[:, None]
    s_hi = fine * jnp.arange(half // fine, dtype=jnp.int32)[None, :]
    ang_hi = (((k * s_hi) % seq).astype(F32) * (2.0 * jnp.pi / seq))[:, :, None]
    ang_lo = angles(pad_rows, fine, seq)[:, None, :]
    cos_p = (jnp.cos(ang_hi) * jnp.cos(ang_lo) - jnp.sin(ang_hi) * jnp.sin(ang_lo))
    sin_p = (jnp.sin(ang_hi) * jnp.cos(ang_lo) + jnp.cos(ang_hi) * jnp.sin(ang_lo))
    live = jnp.arange(pad_rows)[:, None] <= half
    ch = jnp.where(live, cos_p.reshape(pad_rows, half), 0.0) * (seq ** -0.5)
    sh = sin_p.reshape(pad_rows, half)[:half] * (seq ** -0.5)
    a = jnp.arange(BLOCK)[:, None]
    j = jnp.arange(2 * BLOCK)[None, :]
    pm = jnp.where(a == 0, j == BLOCK, j == BLOCK - a)
    return tuple(t.astype(BF16) for t in (cd, sd, ch, sh, pm))


def _fourier(proj, tables, batch, seq, *, u_col0, f_width):
    m = proj.shape[0]
    gd = f_width // N_FOURIER_GROUPS
    half = seq // 2
    cd, sd, ch, sh, pm = tables
    const = lambda shape: _resident(shape, lambda b: (0, 0))
    return pl.pallas_call(
        functools.partial(_fourier_kernel, seq=seq, gd=gd),
        out_shape=jax.ShapeDtypeStruct((m, f_width), BF16),
        grid=(batch,),
        in_specs=[pl.BlockSpec((seq, f_width), lambda b: (b, u_col0 // f_width)),
                  const(cd.shape), const(sd.shape), const(ch.shape), const(sh.shape),
                  const(pm.shape)],
        out_specs=pl.BlockSpec((seq, f_width), lambda b: (b, 0)),
        scratch_shapes=[pltpu.VMEM((half, f_width), BF16) for _ in range(5)],
        name="fourier",
        compiler_params=_params("parallel"),
    )(proj, cd, sd, ch, sh, pm)


def _merge_kernel(a_ref, f_ref, ga_ref, gf_ref, wa_ref, wf_ref, o_ref):
    ta = jnp.dot(a_ref[...], wa_ref[...], preferred_element_type=F32)
    tf = jnp.dot(f_ref[...], wf_ref[...], preferred_element_type=F32)
    o_ref[...] = (ga_ref[...].astype(F32) * ta + gf_ref[...].astype(F32) * tf).astype(BF16)


def _merge(attn, four, projg, wa, wf, *, gate_col0, tm=1024, tn=1024):
    m, aw = attn.shape
    fw = four.shape[1]
    d = wa.shape[1]
    ga0 = gate_col0 // tn
    gf0 = (gate_col0 + d) // tn
    return pl.pallas_call(
        _merge_kernel,
        out_shape=jax.ShapeDtypeStruct((m, d), BF16),
        grid=(m // tm, d // tn),
        in_specs=[pl.BlockSpec((tm, aw), lambda i, j: (i, 0)),
                  pl.BlockSpec((tm, fw), lambda i, j: (i, 0)),
                  pl.BlockSpec((tm, tn), lambda i, j: (i, ga0 + j)),
                  pl.BlockSpec((tm, tn), lambda i, j: (i, gf0 + j)),
                  pl.BlockSpec((aw, tn), lambda i, j: (0, j)),
                  pl.BlockSpec((fw, tn), lambda i, j: (0, j))],
        out_specs=pl.BlockSpec((tm, tn), lambda i, j: (i, j)),
        name="merge",
        compiler_params=_params("parallel", "arbitrary"),
    )(attn, four, projg, projg, wa, wf)


def _out_kernel(m_ref, x_ref, mod_ref, g_ref, wo_ref, x1_ref, h2_ref):
    out = jnp.dot(m_ref[...], wo_ref[...], preferred_element_type=F32)
    x1 = x_ref[...] + mod_ref[0, 2:3, :] * out
    x1_ref[...] = x1
    h2 = _rms_modulate(x1, g_ref[...], mod_ref[0, 4:5, :], mod_ref[0, 3:4, :])
    h2_ref[...] = h2.astype(BF16)


def _out_proj(merged, x2d, mod, b_off, seq, g_ffn, wo, *, tm=512):
    m, d = x2d.shape
    tiles_per_seq = seq // tm
    return pl.pallas_call(
        _out_kernel,
        out_shape=(jax.ShapeDtypeStruct((m, d), F32), jax.ShapeDtypeStruct((m, d), BF16)),
        grid=(m // tm,),
        in_specs=[pl.BlockSpec((tm, d), lambda i: (i, 0)),
                  pl.BlockSpec((tm, d), lambda i: (i, 0)),
                  pl.BlockSpec((1, N_MOD, d), lambda i: (b_off + i // tiles_per_seq, 0, 0)),
                  pl.BlockSpec((1, d), lambda i: (0, 0)),
                  _resident((d, d), lambda i: (0, 0))],
        out_specs=(pl.BlockSpec((tm, d), lambda i: (i, 0)),
                   pl.BlockSpec((tm, d), lambda i: (i, 0))),
        name="out_proj",
        compiler_params=_params("parallel"),
    )(merged, x2d, mod, g_ffn.reshape(1, d), wo)


def _ffn_up_kernel(h_ref, wg_ref, wu_ref, o_ref):
    h = h_ref[...]
    gt = jnp.dot(h, wg_ref[...], preferred_element_type=F32)
    up = jnp.dot(h, wu_ref[...], preferred_element_type=F32)
    o_ref[...] = (gt * jax.nn.sigmoid(gt) * up).astype(BF16)


def _ffn_up(h2, w_up, *, tm=512, tn=2816):
    m, d = h2.shape
    dff = w_up.shape[1] // 2
    nj = dff // tn
    return pl.pallas_call(
        _ffn_up_kernel,
        out_shape=jax.ShapeDtypeStruct((m, dff), BF16),
        grid=(nj, m // tm),
        in_specs=[pl.BlockSpec((tm, d), lambda j, i: (i, 0)),
                  _resident((d, tn), lambda j, i: (0, j)),
                  _resident((d, tn), lambda j, i: (0, nj + j))],
        out_specs=pl.BlockSpec((tm, tn), lambda j, i: (i, j)),
        name="ffn_up",
        compiler_params=_params("arbitrary", "parallel"),
    )(h2, w_up, w_up)


def _ffn_down_kernel(a_ref, x1_ref, mod_ref, g_ref, wd_ref, y_ref, *, final_norm):
    out = jnp.dot(a_ref[...], wd_ref[...], preferred_element_type=F32)
    x2 = x1_ref[...] + mod_ref[0, 5:6, :] * out
    if final_norm:
        y = x2 * lax.rsqrt(jnp.mean(x2 * x2, axis=-1, keepdims=True) + RMS_EPS)
        x2 = y * g_ref[...]
    y_ref[...] = x2


def _ffn_down(act, x1, mod, b_off, seq, g_final, wd, *, final_norm, tm=512):
    m, d = x1.shape
    dff = act.shape[1]
    tiles_per_seq = seq // tm
    return pl.pallas_call(
        functools.partial(_ffn_down_kernel, final_norm=final_norm),
        out_shape=jax.ShapeDtypeStruct((m, d), F32),
        grid=(m // tm,),
        in_specs=[pl.BlockSpec((tm, dff), lambda i: (i, 0)),
                  pl.BlockSpec((tm, d), lambda i: (i, 0)),
                  pl.BlockSpec((1, N_MOD, d), lambda i: (b_off + i // tiles_per_seq, 0, 0)),
                  pl.BlockSpec((1, d), lambda i: (0, 0)),
                  _resident((dff, d), lambda i: (0, 0))],
        out_specs=pl.BlockSpec((tm, d), lambda i: (i, 0)),
        name="ffn_down",
        compiler_params=_params("parallel"),
    )(act, x1, mod, g_final.reshape(1, d), wd)


def kernel(x_prompt, x_sample, c_prompt, c_sample, w_mod, b_mod, g_mix, w_in, attn_sink,
           w_attn_branch, w_fourier_branch, w_gate, b_gate, w_out, g_ffn, w_up, w_down,
           g_final):
    depth, d, in_width = w_in.shape
    attn_width = w_attn_branch.shape[1]
    f_width = w_fourier_branch.shape[1]
    kv_width = (in_width - attn_width - f_width) // 2
    n_heads = attn_width // HEAD_DIM
    groups = [(x_prompt, 0), (x_sample, c_prompt.shape[0])]
    seq = x_prompt.shape[1]
    assert x_sample.shape[1] == seq and seq % (4 * BLOCK) == 0
    assert (attn_width + 2 * kv_width) % f_width == 0
    assert w_gate.shape[2] == in_width

    c_all = jnp.concatenate([c_prompt, c_sample], axis=0)
    pad = -c_all.shape[0] % BF16_SUBLANE_TILE
    c_all = jnp.pad(c_all, ((0, pad), (0, 0)))

    heads = jnp.arange(1, n_heads + 1, dtype=F32)
    slopes = jnp.exp2(-8.0 * heads / n_heads)
    dft_tables = _dft_tables(seq, f_width // N_FOURIER_GROUPS)

    xs = [x.reshape(-1, d) for x, _ in groups]
    for l in range(depth):
        last = l == depth - 1
        mod = _modulation(c_all, w_mod[l], b_mod[l]).reshape(-1, N_MOD, d)
        w_pair = jnp.stack([w_in[l], w_gate[l]]).astype(BF16)
        wa = w_attn_branch[l].astype(BF16)
        wf = w_fourier_branch[l].astype(BF16)
        wo = w_out[l].astype(BF16)
        wu = w_up[l].astype(BF16)
        wd = w_down[l].astype(BF16)
        sink = attn_sink[l].astype(F32)
        new_xs = []
        for x2d, (x_in, b_off) in zip(xs, groups):
            batch = x_in.shape[0]
            projg = _projection(x2d, mod, b_off, seq, g_mix[l], w_pair, b_gate[l])
            attn = _attention(projg, slopes, sink, batch, seq,
                              attn_width=attn_width, kv_width=kv_width)
            four = _fourier(projg, dft_tables, batch, seq,
                            u_col0=attn_width + 2 * kv_width, f_width=f_width)
            merged = _merge(attn, four, projg, wa, wf, gate_col0=in_width)
            x1, h2 = _out_proj(merged, x2d, mod, b_off, seq, g_ffn[l], wo)
            act = _ffn_up(h2, wu)
            new_xs.append(_ffn_down(act, x1, mod, b_off, seq, g_final, wd, final_norm=last))
        xs = new_xs
    return tuple(x2d.reshape(x_in.shape) for x2d, (x_in, _) in zip(xs, groups))
```

```python
import functools
import math

import jax
import jax.numpy as jnp
from jax import lax
from jax.experimental import pallas as pl
from jax.experimental.pallas import tpu as pltpu

F32 = jnp.float32
BF16 = jnp.bfloat16

HEAD_DIM = 128
Q_PER_KV = 4
N_FOURIER_GROUPS = 4
WINDOW = 128
BLOCK = 128
N_MOD = 6
RMS_EPS = 1e-6
LOG2E = math.log2(math.e)

V7X_VMEM_LIMIT_BYTES = 60 * 1024 * 1024
BF16_SUBLANE_TILE = 16


def _params(*semantics):
    return pltpu.CompilerParams(dimension_semantics=semantics,
                                vmem_limit_bytes=V7X_VMEM_LIMIT_BYTES)


def _resident(block_shape, index_map):
    return pl.BlockSpec(block_shape, index_map, pipeline_mode=pl.Buffered(1))


def _rms_modulate(x, gain, scale, shift):
    y = x * lax.rsqrt(jnp.mean(x * x, axis=-1, keepdims=True) + RMS_EPS)
    return y * (gain * (1.0 + scale)) + shift


def _mod_kernel(c_ref, w_ref, b_ref, o_ref):
    c = c_ref[...]
    a = (c * jax.nn.sigmoid(c)).astype(BF16)
    o_ref[...] = jnp.dot(a, w_ref[...].astype(BF16), preferred_element_type=F32) + b_ref[...]


def _modulation(c, w_mod, b_mod, *, tn=512):
    bp, d = c.shape
    n = w_mod.shape[1]
    return pl.pallas_call(
        _mod_kernel,
        out_shape=jax.ShapeDtypeStruct((bp, n), F32),
        grid=(n // tn,),
        in_specs=[pl.BlockSpec((bp, d), lambda j: (0, 0)),
                  pl.BlockSpec((d, tn), lambda j: (0, j)),
                  pl.BlockSpec((1, tn), lambda j: (0, j))],
        out_specs=pl.BlockSpec((bp, tn), lambda j: (0, j)),
        name="modulation",
        compiler_params=_params("arbitrary"),
    )(c, w_mod, b_mod.reshape(1, n))


def _proj_kernel(x0_ref, xn_ref, mod0_ref, modn_ref, g_ref, w_ref, b_ref, o_ref,
                 h_even_ref, h_odd_ref, *, n_in_tiles, n_col_tiles, row_chunk):
    i = pl.program_id(0)
    j = pl.program_id(1)
    odd = lax.rem(i, 2) == 1
    gain = g_ref[...]
    tm = xn_ref.shape[0]
    rows_per_step = tm // n_col_tiles

    def norm_rows(x_ref, mod_ref, dst_ref, row0, n_rows):
        rows = pl.ds(pl.multiple_of(row0, n_rows), n_rows)
        h = _rms_modulate(x_ref[rows, :], gain, mod_ref[0, 1:2, :], mod_ref[0, 0:1, :])
        dst_ref[rows, :] = h.astype(BF16)

    @pl.when((i == 0) & (j == 0))
    def _():
        def body(r, carry):
            norm_rows(x0_ref, mod0_ref, h_even_ref, r * row_chunk, row_chunk)
            return carry

        lax.fori_loop(0, tm // row_chunk, body, 0)

    def column_step(cur_ref, nxt_ref, gated):
        norm_rows(xn_ref, modn_ref, nxt_ref, j * rows_per_step, rows_per_step)
        acc = jnp.dot(cur_ref[...], w_ref[...], preferred_element_type=F32)
        if gated:
            acc = 0.5 * jnp.tanh(0.5 * (acc + b_ref[...])) + 0.5
        o_ref[...] = acc.astype(BF16)

    for is_odd, cur_ref, nxt_ref in ((False, h_even_ref, h_odd_ref),
                                     (True, h_odd_ref, h_even_ref)):
        for gated in (False, True):
            pl.when((odd == is_odd) & ((j >= n_in_tiles) == gated))(
                functools.partial(column_step, cur_ref, nxt_ref, gated))


def _projection(x2d, mod, b_off, seq, g_mix, w_pair, b_gate, *, tm=1024, tn=1024):
    m, d = x2d.shape
    n = 2 * w_pair.shape[2]
    n_in_tiles = w_pair.shape[2] // tn
    n_col_tiles = n // tn
    tiles_per_seq = seq // tm
    last = m // tm - 1
    kern = functools.partial(_proj_kernel, n_in_tiles=n_in_tiles, n_col_tiles=n_col_tiles,
                             row_chunk=256)
    nxt = lambda i: jnp.minimum(i + 1, last)
    return pl.pallas_call(
        kern,
        out_shape=jax.ShapeDtypeStruct((m, n), BF16),
        grid=(m // tm, n_col_tiles),
        in_specs=[_resident((tm, d), lambda i, j: (0, 0)),
                  pl.BlockSpec((tm, d), lambda i, j: (nxt(i), 0)),
                  _resident((1, N_MOD, d), lambda i, j: (b_off, 0, 0)),
                  pl.BlockSpec((1, N_MOD, d),
                               lambda i, j: (b_off + nxt(i) // tiles_per_seq, 0, 0)),
                  pl.BlockSpec((1, d), lambda i, j: (0, 0)),
                  pl.BlockSpec((None, d, tn),
                               lambda i, j: (j // n_in_tiles, 0, lax.rem(j, n_in_tiles))),
                  pl.BlockSpec((1, tn), lambda i, j: (0, jnp.maximum(j - n_in_tiles, 0)))],
        out_specs=pl.BlockSpec((tm, tn), lambda i, j: (i, j)),
        scratch_shapes=[pltpu.VMEM((tm, d), BF16), pltpu.VMEM((tm, d), BF16)],
        name="projection",
        compiler_params=_params("arbitrary", "arbitrary"),
    )(x2d, x2d, mod, mod, g_mix.reshape(1, d), w_pair, b_gate.reshape(1, -1))


def _attn_kernel(slopes_ref, sink_ref, q_ref, k_ref, v_ref, o_ref,
                 bias_ref, v1_ref, s0_ref, s1_ref, p0_ref, p1_ref, e0_ref, e1_ref, *, seq):
    s_refs, p_refs, e_refs = (s0_ref, s1_ref), (p0_ref, p1_ref), (e0_ref, e1_ref)
    kh = pl.program_id(1)
    n_chunks = 3
    span = n_chunks * BLOCK
    n_blocks = seq // BLOCK
    logit_scale = (HEAD_DIM ** -0.5) * LOG2E

    @pl.when(pl.program_id(0) == 0)
    def _():
        q_pos = lax.broadcasted_iota(jnp.int32, (BLOCK, BLOCK), 0)
        k_pos = lax.broadcasted_iota(jnp.int32, (BLOCK, BLOCK), 1)
        for d in range(-2, 3):
            rel = d * BLOCK + k_pos - q_pos
            valid = jnp.abs(rel) <= WINDOW
            dist = jnp.abs(rel).astype(F32)
            for g in range(Q_PER_KV):
                slope2 = slopes_ref[kh * Q_PER_KV + g] * LOG2E
                bias_ref[kh, d + 2, g * BLOCK:(g + 1) * BLOCK, :] = jnp.where(
                    valid, -slope2 * dist, -jnp.inf)
        v1_ref[:, HEAD_DIM:] = jnp.ones((seq, HEAD_DIM), BF16)

    sinks2 = [sink_ref[kh * Q_PER_KV + g] * LOG2E for g in range(Q_PER_KV)]
    v1_ref[:, :HEAD_DIM] = v_ref[...]

    def window(i):
        q0 = pl.multiple_of(i * BLOCK, BLOCK)
        k0 = pl.multiple_of(jnp.clip(q0 - BLOCK, 0, seq - span), BLOCK)
        return q0, k0

    def logits(i, s_ref):
        q0, k0 = window(i)
        first = lax.div(k0 - q0 + 2 * BLOCK, BLOCK)
        q = q_ref[pl.ds(q0, BLOCK), :]
        qs = jnp.concatenate([q[:, g * HEAD_DIM:(g + 1) * HEAD_DIM] for g in range(Q_PER_KV)],
                             axis=0)
        kw = k_ref[pl.ds(k0, span), :]
        s = lax.dot_general(qs, kw, (((1,), (1,)), ((), ())), preferred_element_type=F32)
        for t in range(n_chunks):
            cols = slice(t * BLOCK, (t + 1) * BLOCK)
            s_ref[:, cols] = s[:, cols] * logit_scale + bias_ref[kh, first + t]

    def softmax(s_ref, p_ref, e_ref):
        for g in range(Q_PER_KV):
            rows = slice(g * BLOCK, (g + 1) * BLOCK)
            a = s_ref[rows, :]
            mx = jnp.maximum(jnp.max(a, axis=-1, keepdims=True), sinks2[g])
            p_ref[rows, :] = jnp.exp2(a - mx).astype(BF16)
            e_ref[rows, :] = jnp.broadcast_to(jnp.exp2(sinks2[g] - mx), (BLOCK, HEAD_DIM))

    def values(i, p_ref, e_ref):
        q0, k0 = window(i)
        r = jnp.dot(p_ref[...], v1_ref[pl.ds(k0, span), :], preferred_element_type=F32)
        out = (r[:, :HEAD_DIM] / (r[:, HEAD_DIM:] + e_ref[...])).astype(BF16)
        for g in range(Q_PER_KV):
            o_ref[pl.ds(q0, BLOCK), g * HEAD_DIM:(g + 1) * HEAD_DIM] = (
                out[g * BLOCK:(g + 1) * BLOCK])

    def pair(i, n_logits, n_softmax):
        for half in range(2):
            blk = i + half
            if half < n_logits:
                logits(blk + 2, s_refs[half])
            if half < n_softmax:
                softmax(s_refs[1 - half], p_refs[1 - half], e_refs[1 - half])
            values(blk, p_refs[half], e_refs[half])

    logits(0, s_refs[0])
    logits(1, s_refs[1])
    softmax(s_refs[0], p_refs[0], e_refs[0])

    def body(it, carry):
        pair(4 * it, 2, 2)
        pair(4 * it + 2, 2, 2)
        return carry

    full_pairs = n_blocks // 2 - 1
    lax.fori_loop(0, full_pairs // 2, body, 0)
    if full_pairs % 2:
        pair(n_blocks - 4, 2, 2)
    pair(n_blocks - 2, 0, 1)


def _attention(proj, slopes, sink, batch, seq, *, attn_width, kv_width):
    m = proj.shape[0]
    n_kv = kv_width // HEAD_DIM
    gw = Q_PER_KV * HEAD_DIM
    k_col0 = attn_width // HEAD_DIM
    v_col0 = (attn_width + kv_width) // HEAD_DIM
    rows = Q_PER_KV * BLOCK
    grid_spec = pltpu.PrefetchScalarGridSpec(
        num_scalar_prefetch=2,
        grid=(batch, n_kv),
        in_specs=[pl.BlockSpec((seq, gw), lambda b, h, *_: (b, h)),
                  pl.BlockSpec((seq, HEAD_DIM), lambda b, h, *_: (b, k_col0 + h)),
                  pl.BlockSpec((seq, HEAD_DIM), lambda b, h, *_: (b, v_col0 + h))],
        out_specs=pl.BlockSpec((seq, gw), lambda b, h, *_: (b, h)),
        scratch_shapes=[pltpu.VMEM((n_kv, 5, rows, BLOCK), F32),
                        pltpu.VMEM((seq, 2 * HEAD_DIM), BF16),
                        *[pltpu.VMEM((rows, 3 * BLOCK), F32) for _ in range(2)],
                        *[pltpu.VMEM((rows, 3 * BLOCK), BF16) for _ in range(2)],
                        *[pltpu.VMEM((rows, HEAD_DIM), F32) for _ in range(2)]])
    return pl.pallas_call(
        functools.partial(_attn_kernel, seq=seq),
        out_shape=jax.ShapeDtypeStruct((m, attn_width), BF16),
        grid_spec=grid_spec,
        name="attention",
        compiler_params=_params("arbitrary", "arbitrary"),
    )(slopes, sink, proj, proj, proj)


def _fourier_kernel(u_ref, cd_ref, sd_ref, ch_ref, sh_ref, pm_ref, o_ref,
                    ue_ref, uo_ref, ae_ref, bo_ref, y_ref, *, seq, gd):
    half = seq // 2
    n_blocks = half // BLOCK
    pm = pm_ref[...]

    def reversed_block(src_ref, b, end):
        lo = end - (b + 1) * BLOCK
        if b == 0:
            return jnp.dot(pm[:, :BLOCK], src_ref[lo:end, :], preferred_element_type=F32)
        return jnp.dot(pm, src_ref[lo:lo + 2 * BLOCK, :], preferred_element_type=F32)

    for b in range(n_blocks):
        rows = slice(b * BLOCK, (b + 1) * BLOCK)
        x = u_ref[rows, :].astype(F32)
        r = reversed_block(u_ref, b, seq)
        ue_ref[rows, :] = (x + r).astype(BF16)
        uo_ref[rows, :] = (x - r).astype(BF16)

    mid = []
    for g in range(N_FOURIER_GROUPS):
        cols = slice(g * gd, (g + 1) * gd)
        ae_ref[:, cols] = jnp.dot(ue_ref[:, cols], cd_ref[...],
                                  preferred_element_type=F32).astype(BF16)
        bo_ref[:, cols] = jnp.dot(uo_ref[:, cols], sd_ref[...],
                                  preferred_element_type=F32).astype(BF16)
        mid.append(jnp.dot(u_ref[half:half + BF16_SUBLANE_TILE, cols], cd_ref[...],
                           preferred_element_type=F32)[0:1])
    a_mid = jnp.concatenate(mid, axis=1) * (seq ** -0.5)

    p = jnp.dot(ch_ref[...], ae_ref[...], preferred_element_type=F32)
    q = jnp.dot(sh_ref[...], bo_ref[...], preferred_element_type=F32)
    k_idx = lax.broadcasted_iota(jnp.int32, (half, 1), 0)
    sign = 1.0 - 2.0 * (k_idx & 1).astype(F32)
    base = p[:half] + sign * a_mid
    o_ref[0:half, :] = (base - q).astype(BF16)
    y_ref[...] = (base + q).astype(BF16)
    z_mid = p[half:half + 1] + a_mid

    for b in range(n_blocks):
        r = reversed_block(y_ref, b, half)
        if b == 0:
            first_row = lax.broadcasted_iota(jnp.int32, (BLOCK, 1), 0) == 0
            r = jnp.where(first_row, z_mid, r)
        o_ref[half + b * BLOCK:half + (b + 1) * BLOCK, :] = r.astype(BF16)


def _dft_tables(seq, gd):
    half = seq // 2

    def angles(rows, cols, n):
        k = jnp.arange(rows, dtype=jnp.int32)[:, None]
        s = jnp.arange(cols, dtype=jnp.int32)[None, :]
        return ((k * s) % n).astype(F32) * (2.0 * jnp.pi / n)

    ang_c = angles(gd, gd, gd)
    cd = jnp.cos(ang_c) * (gd ** -0.5)
    sd = jnp.sin(ang_c) * (gd ** -0.5)
    pad_rows = half + BF16_SUBLANE_TILE
    fine = 32
    k = jnp.arange(pad_rows, dtype=jnp.int32)[:, None]
    s_hi = fine * jnp.arange(half // fine, dtype=jnp.int32)[None, :]
    ang_hi = (((k * s_hi) % seq).astype(F32) * (2.0 * jnp.pi / seq))[:, :, None]
    ang_lo = angles(pad_rows, fine, seq)[:, None, :]
    cos_p = (jnp.cos(ang_hi) * jnp.cos(ang_lo) - jnp.sin(ang_hi) * jnp.sin(ang_lo))
    sin_p = (jnp.sin(ang_hi) * jnp.cos(ang_lo) + jnp.cos(ang_hi) * jnp.sin(ang_lo))
    live = jnp.arange(pad_rows)[:, None] <= half
    ch = jnp.where(live, cos_p.reshape(pad_rows, half), 0.0) * (seq ** -0.5)
    sh = sin_p.reshape(pad_rows, half)[:half] * (seq ** -0.5)
    a = jnp.arange(BLOCK)[:, None]
    j = jnp.arange(2 * BLOCK)[None, :]
    pm = jnp.where(a == 0, j == BLOCK, j == BLOCK - a)
    return tuple(t.astype(BF16) for t in (cd, sd, ch, sh, pm))


def _fourier(proj, tables, batch, seq, *, u_col0, f_width):
    m = proj.shape[0]
    gd = f_width // N_FOURIER_GROUPS
    half = seq // 2
    cd, sd, ch, sh, pm = tables
    const = lambda shape: _resident(shape, lambda b: (0, 0))
    return pl.pallas_call(
        functools.partial(_fourier_kernel, seq=seq, gd=gd),
        out_shape=jax.ShapeDtypeStruct((m, f_width), BF16),
        grid=(batch,),
        in_specs=[pl.BlockSpec((seq, f_width), lambda b: (b, u_col0 // f_width)),
                  const(cd.shape), const(sd.shape), const(ch.shape), const(sh.shape),
                  const(pm.shape)],
        out_specs=pl.BlockSpec((seq, f_width), lambda b: (b, 0)),
        scratch_shapes=[pltpu.VMEM((half, f_width), BF16) for _ in range(5)],
        name="fourier",
        compiler_params=_params("parallel"),
    )(proj, cd, sd, ch, sh, pm)


def _merge_kernel(a_ref, f_ref, ga_ref, gf_ref, wa_ref, wf_ref, o_ref):
    ta = jnp.dot(a_ref[...], wa_ref[...], preferred_element_type=F32)
    tf = jnp.dot(f_ref[...], wf_ref[...], preferred_element_type=F32)
    o_ref[...] = (ga_ref[...].astype(F32) * ta + gf_ref[...].astype(F32) * tf).astype(BF16)


def _merge(attn, four, projg, wa, wf, *, gate_col0, tm=1024, tn=1024):
    m, aw = attn.shape
    fw = four.shape[1]
    d = wa.shape[1]
    ga0 = gate_col0 // tn
    gf0 = (gate_col0 + d) // tn
    return pl.pallas_call(
        _merge_kernel,
        out_shape=jax.ShapeDtypeStruct((m, d), BF16),
        grid=(m // tm, d // tn),
        in_specs=[pl.BlockSpec((tm, aw), lambda i, j: (i, 0)),
                  pl.BlockSpec((tm, fw), lambda i, j: (i, 0)),
                  pl.BlockSpec((tm, tn), lambda i, j: (i, ga0 + j)),
                  pl.BlockSpec((tm, tn), lambda i, j: (i, gf0 + j)),
                  pl.BlockSpec((aw, tn), lambda i, j: (0, j)),
                  pl.BlockSpec((fw, tn), lambda i, j: (0, j))],
        out_specs=pl.BlockSpec((tm, tn), lambda i, j: (i, j)),
        name="merge",
        compiler_params=_params("parallel", "arbitrary"),
    )(attn, four, projg, projg, wa, wf)


def _out_kernel(m_ref, x_ref, mod_ref, g_ref, wo_ref, x1_ref, h2_ref):
    out = jnp.dot(m_ref[...], wo_ref[...], preferred_element_type=F32)
    x1 = x_ref[...] + mod_ref[0, 2:3, :] * out
    x1_ref[...] = x1
    h2 = _rms_modulate(x1, g_ref[...], mod_ref[0, 4:5, :], mod_ref[0, 3:4, :])
    h2_ref[...] = h2.astype(BF16)


def _out_proj(merged, x2d, mod, b_off, seq, g_ffn, wo, *, tm=512):
    m, d = x2d.shape
    tiles_per_seq = seq // tm
    return pl.pallas_call(
        _out_kernel,
        out_shape=(jax.ShapeDtypeStruct((m, d), F32), jax.ShapeDtypeStruct((m, d), BF16)),
        grid=(m // tm,),
        in_specs=[pl.BlockSpec((tm, d), lambda i: (i, 0)),
                  pl.BlockSpec((tm, d), lambda i: (i, 0)),
                  pl.BlockSpec((1, N_MOD, d), lambda i: (b_off + i // tiles_per_seq, 0, 0)),
                  pl.BlockSpec((1, d), lambda i: (0, 0)),
                  _resident((d, d), lambda i: (0, 0))],
        out_specs=(pl.BlockSpec((tm, d), lambda i: (i, 0)),
                   pl.BlockSpec((tm, d), lambda i: (i, 0))),
        name="out_proj",
        compiler_params=_params("parallel"),
    )(merged, x2d, mod, g_ffn.reshape(1, d), wo)


def _ffn_up_kernel(h_ref, wg_ref, wu_ref, o_ref):
    h = h_ref[...]
    gt = jnp.dot(h, wg_ref[...], preferred_element_type=F32)
    up = jnp.dot(h, wu_ref[...], preferred_element_type=F32)
    o_ref[...] = (gt * jax.nn.sigmoid(gt) * up).astype(BF16)


def _ffn_up(h2, w_up, *, tm=512, tn=2816):
    m, d = h2.shape
    dff = w_up.shape[1] // 2
    nj = dff // tn
    return pl.pallas_call(
        _ffn_up_kernel,
        out_shape=jax.ShapeDtypeStruct((m, dff), BF16),
        grid=(nj, m // tm),
        in_specs=[pl.BlockSpec((tm, d), lambda j, i: (i, 0)),
                  _resident((d, tn), lambda j, i: (0, j)),
                  _resident((d, tn), lambda j, i: (0, nj + j))],
        out_specs=pl.BlockSpec((tm, tn), lambda j, i: (i, j)),
        name="ffn_up",
        compiler_params=_params("arbitrary", "parallel"),
    )(h2, w_up, w_up)


def _ffn_down_kernel(a_ref, x1_ref, mod_ref, g_ref, wd_ref, y_ref, *, final_norm):
    out = jnp.dot(a_ref[...], wd_ref[...], preferred_element_type=F32)
    x2 = x1_ref[...] + mod_ref[0, 5:6, :] * out
    if final_norm:
        y = x2 * lax.rsqrt(jnp.mean(x2 * x2, axis=-1, keepdims=True) + RMS_EPS)
        x2 = y * g_ref[...]
    y_ref[...] = x2


def _ffn_down(act, x1, mod, b_off, seq, g_final, wd, *, final_norm, tm=512):
    m, d = x1.shape
    dff = act.shape[1]
    tiles_per_seq = seq // tm
    return pl.pallas_call(
        functools.partial(_ffn_down_kernel, final_norm=final_norm),
        out_shape=jax.ShapeDtypeStruct((m, d), F32),
        grid=(m // tm,),
        in_specs=[pl.BlockSpec((tm, dff), lambda i: (i, 0)),
                  pl.BlockSpec((tm, d), lambda i: (i, 0)),
                  pl.BlockSpec((1, N_MOD, d), lambda i: (b_off + i // tiles_per_seq, 0, 0)),
                  pl.BlockSpec((1, d), lambda i: (0, 0)),
                  _resident((dff, d), lambda i: (0, 0))],
        out_specs=pl.BlockSpec((tm, d), lambda i: (i, 0)),
        name="ffn_down",
        compiler_params=_params("parallel"),
    )(act, x1, mod, g_final.reshape(1, d), wd)


def kernel(x_prompt, x_sample, c_prompt, c_sample, w_mod, b_mod, g_mix, w_in, attn_sink,
           w_attn_branch, w_fourier_branch, w_gate, b_gate, w_out, g_ffn, w_up, w_down,
           g_final):
    depth, d, in_width = w_in.shape
    attn_width = w_attn_branch.shape[1]
    f_width = w_fourier_branch.shape[1]
    kv_width = (in_width - attn_width - f_width) // 2
    n_heads = attn_width // HEAD_DIM
    groups = [(x_prompt, 0), (x_sample, c_prompt.shape[0])]
    seq = x_prompt.shape[1]
    assert x_sample.shape[1] == seq and seq % (4 * BLOCK) == 0
    assert (attn_width + 2 * kv_width) % f_width == 0
    assert w_gate.shape[2] == in_width

    c_all = jnp.concatenate([c_prompt, c_sample], axis=0)
    pad = -c_all.shape[0] % BF16_SUBLANE_TILE
    c_all = jnp.pad(c_all, ((0, pad), (0, 0)))

    heads = jnp.arange(1, n_heads + 1, dtype=F32)
    slopes = jnp.exp2(-8.0 * heads / n_heads)
    dft_tables = _dft_tables(seq, f_width // N_FOURIER_GROUPS)

    xs = [x.reshape(-1, d) for x, _ in groups]
    for l in range(depth):
        last = l == depth - 1
        mod = _modulation(c_all, w_mod[l], b_mod[l]).reshape(-1, N_MOD, d)
        w_pair = jnp.stack([w_in[l], w_gate[l]]).astype(BF16)
        wa = w_attn_branch[l].astype(BF16)
        wf = w_fourier_branch[l].astype(BF16)
        wo = w_out[l].astype(BF16)
        wu = w_up[l].astype(BF16)
        wd = w_down[l].astype(BF16)
        sink = attn_sink[l].astype(F32)
        new_xs = []
        for x2d, (x_in, b_off) in zip(xs, groups):
            batch = x_in.shape[0]
            projg = _projection(x2d, mod, b_off, seq, g_mix[l], w_pair, b_gate[l])
            attn = _attention(projg, slopes, sink, batch, seq,
                              attn_width=attn_width, kv_width=kv_width)
            four = _fourier(projg, dft_tables, batch, seq,
                            u_col0=attn_width + 2 * kv_width, f_width=f_width)
            merged = _merge(attn, four, projg, wa, wf, gate_col0=in_width)
            x1, h2 = _out_proj(merged, x2d, mod, b_off, seq, g_ffn[l], wo)
            act = _ffn_up(h2, wu)
            new_xs.append(_ffn_down(act, x1, mod, b_off, seq, g_final, wd, final_norm=last))
        xs = new_xs
    return tuple(x2d.reshape(x_in.shape) for x2d, (x_in, _) in zip(xs, groups))
```

```python
import functools
import math

import jax
import jax.numpy as jnp
from jax import lax
from jax.experimental import pallas as pl
from jax.experimental.pallas import tpu as pltpu

F32 = jnp.float32
BF16 = jnp.bfloat16

HEAD_DIM = 128
Q_PER_KV = 4
N_FOURIER_GROUPS = 4
WINDOW = 128
BLOCK = 128
N_MOD = 6
RMS_EPS = 1e-6
LOG2E = math.log2(math.e)

V7X_VMEM_LIMIT_BYTES = 60 * 1024 * 1024
BF16_SUBLANE_TILE = 16


def _params(*semantics):
    return pltpu.CompilerParams(dimension_semantics=semantics,
                                vmem_limit_bytes=V7X_VMEM_LIMIT_BYTES)


def _resident(block_shape, index_map):
    return pl.BlockSpec(block_shape, index_map, pipeline_mode=pl.Buffered(1))


def _rms_modulate(x, gain, scale, shift):
    y = x * lax.rsqrt(jnp.mean(x * x, axis=-1, keepdims=True) + RMS_EPS)
    return y * (gain * (1.0 + scale)) + shift


def _mod_kernel(c_ref, w_ref, b_ref, o_ref):
    c = c_ref[...]
    a = (c * jax.nn.sigmoid(c)).astype(BF16)
    o_ref[...] = jnp.dot(a, w_ref[...].astype(BF16), preferred_element_type=F32) + b_ref[...]


def _modulation(c, w_mod, b_mod, *, tn=512):
    bp, d = c.shape
    n = w_mod.shape[1]
    return pl.pallas_call(
        _mod_kernel,
        out_shape=jax.ShapeDtypeStruct((bp, n), F32),
        grid=(n // tn,),
        in_specs=[pl.BlockSpec((bp, d), lambda j: (0, 0)),
                  pl.BlockSpec((d, tn), lambda j: (0, j)),
                  pl.BlockSpec((1, tn), lambda j: (0, j))],
        out_specs=pl.BlockSpec((bp, tn), lambda j: (0, j)),
        name="modulation",
        compiler_params=_params("arbitrary"),
    )(c, w_mod, b_mod.reshape(1, n))


def _proj_kernel(x0_ref, xn_ref, mod0_ref, modn_ref, g_ref, w_ref, b_ref, o_ref,
                 h_even_ref, h_odd_ref, *, n_in_tiles, n_col_tiles, row_chunk):
    i = pl.program_id(0)
    j = pl.program_id(1)
    odd = lax.rem(i, 2) == 1
    gain = g_ref[...]
    tm = xn_ref.shape[0]
    rows_per_step = tm // n_col_tiles

    def norm_rows(x_ref, mod_ref, dst_ref, row0, n_rows):
        rows = pl.ds(pl.multiple_of(row0, n_rows), n_rows)
        h = _rms_modulate(x_ref[rows, :], gain, mod_ref[0, 1:2, :], mod_ref[0, 0:1, :])
        dst_ref[rows, :] = h.astype(BF16)

    @pl.when((i == 0) & (j == 0))
    def _():
        def body(r, carry):
            norm_rows(x0_ref, mod0_ref, h_even_ref, r * row_chunk, row_chunk)
            return carry

        lax.fori_loop(0, tm // row_chunk, body, 0)

    def column_step(cur_ref, nxt_ref, gated):
        norm_rows(xn_ref, modn_ref, nxt_ref, j * rows_per_step, rows_per_step)
        acc = jnp.dot(cur_ref[...], w_ref[...], preferred_element_type=F32)
        if gated:
            acc = 0.5 * jnp.tanh(0.5 * (acc + b_ref[...])) + 0.5
        o_ref[...] = acc.astype(BF16)

    for is_odd, cur_ref, nxt_ref in ((False, h_even_ref, h_odd_ref),
                                     (True, h_odd_ref, h_even_ref)):
        for gated in (False, True):
            pl.when((odd == is_odd) & ((j >= n_in_tiles) == gated))(
                functools.partial(column_step, cur_ref, nxt_ref, gated))


def _projection(x2d, mod, b_off, seq, g_mix, w_pair, b_gate, *, tm=1024, tn=1024):
    m, d = x2d.shape
    n = 2 * w_pair.shape[2]
    n_in_tiles = w_pair.shape[2] // tn
    n_col_tiles = n // tn
    tiles_per_seq = seq // tm
    last = m // tm - 1
    kern = functools.partial(_proj_kernel, n_in_tiles=n_in_tiles, n_col_tiles=n_col_tiles,
                             row_chunk=256)
    nxt = lambda i: jnp.minimum(i + 1, last)
    return pl.pallas_call(
        kern,
        out_shape=jax.ShapeDtypeStruct((m, n), BF16),
        grid=(m // tm, n_col_tiles),
        in_specs=[_resident((tm, d), lambda i, j: (0, 0)),
                  pl.BlockSpec((tm, d), lambda i, j: (nxt(i), 0)),
                  _resident((1, N_MOD, d), lambda i, j: (b_off, 0, 0)),
                  pl.BlockSpec((1, N_MOD, d),
                               lambda i, j: (b_off + nxt(i) // tiles_per_seq, 0, 0)),
                  pl.BlockSpec((1, d), lambda i, j: (0, 0)),
                  pl.BlockSpec((None, d, tn),
                               lambda i, j: (j // n_in_tiles, 0, lax.rem(j, n_in_tiles))),
                  pl.BlockSpec((1, tn), lambda i, j: (0, jnp.maximum(j - n_in_tiles, 0)))],
        out_specs=pl.BlockSpec((tm, tn), lambda i, j: (i, j)),
        scratch_shapes=[pltpu.VMEM((tm, d), BF16), pltpu.VMEM((tm, d), BF16)],
        name="projection",
        compiler_params=_params("arbitrary", "arbitrary"),
    )(x2d, x2d, mod, mod, g_mix.reshape(1, d), w_pair, b_gate.reshape(1, -1))


def _attn_kernel(slopes_ref, sink_ref, q_ref, k_ref, v_ref, o_ref,
                 bias_ref, *stage_refs, seq, heads_per_step):
    per_head = [stage_refs[7 * j:7 * (j + 1)] for j in range(heads_per_step)]
    first_head = pl.program_id(1) * heads_per_step
    n_chunks = 3
    span = n_chunks * BLOCK
    n_blocks = seq // BLOCK
    gw = Q_PER_KV * HEAD_DIM
    logit_scale = (HEAD_DIM ** -0.5) * LOG2E

    @pl.when(pl.program_id(0) == 0)
    def _():
        q_pos = lax.broadcasted_iota(jnp.int32, (BLOCK, BLOCK), 0)
        k_pos = lax.broadcasted_iota(jnp.int32, (BLOCK, BLOCK), 1)
        for d in range(-2, 3):
            rel = d * BLOCK + k_pos - q_pos
            valid = jnp.abs(rel) <= WINDOW
            dist = jnp.abs(rel).astype(F32)
            for j in range(heads_per_step):
                for g in range(Q_PER_KV):
                    slope2 = slopes_ref[(first_head + j) * Q_PER_KV + g] * LOG2E
                    bias_ref[first_head + j, d + 2, g * BLOCK:(g + 1) * BLOCK, :] = jnp.where(
                        valid, -slope2 * dist, -jnp.inf)
        for refs in per_head:
            refs[0][:, HEAD_DIM:] = jnp.ones((seq, HEAD_DIM), BF16)

    def window(i):
        q0 = pl.multiple_of(i * BLOCK, BLOCK)
        k0 = pl.multiple_of(jnp.clip(q0 - BLOCK, 0, seq - span), BLOCK)
        return q0, k0

    def one_kv_head(j):
        kh = first_head + j
        v1_ref, s0_ref, s1_ref, p0_ref, p1_ref, e0_ref, e1_ref = per_head[j]
        s_refs, p_refs, e_refs = (s0_ref, s1_ref), (p0_ref, p1_ref), (e0_ref, e1_ref)
        sinks2 = [sink_ref[kh * Q_PER_KV + g] * LOG2E for g in range(Q_PER_KV)]
        head_cols = slice(j * HEAD_DIM, (j + 1) * HEAD_DIM)
        v1_ref[:, :HEAD_DIM] = v_ref[:, head_cols]

        def logits(i, s_ref):
            q0, k0 = window(i)
            first = lax.div(k0 - q0 + 2 * BLOCK, BLOCK)
            q = q_ref[pl.ds(q0, BLOCK), j * gw:(j + 1) * gw]
            qs = jnp.concatenate([q[:, g * HEAD_DIM:(g + 1) * HEAD_DIM]
                                  for g in range(Q_PER_KV)], axis=0)
            kw = k_ref[pl.ds(k0, span), head_cols]
            s = lax.dot_general(qs, kw, (((1,), (1,)), ((), ())), preferred_element_type=F32)
            for t in range(n_chunks):
                cols = slice(t * BLOCK, (t + 1) * BLOCK)
                s_ref[:, cols] = s[:, cols] * logit_scale + bias_ref[kh, first + t]

        def softmax(s_ref, p_ref, e_ref):
            for g in range(Q_PER_KV):
                rows = slice(g * BLOCK, (g + 1) * BLOCK)
                a = s_ref[rows, :]
                mx = jnp.maximum(jnp.max(a, axis=-1, keepdims=True), sinks2[g])
                p_ref[rows, :] = jnp.exp2(a - mx).astype(BF16)
                e_ref[rows, :] = jnp.broadcast_to(jnp.exp2(sinks2[g] - mx), (BLOCK, HEAD_DIM))

        def values(i, p_ref, e_ref):
            q0, k0 = window(i)
            r = jnp.dot(p_ref[...], v1_ref[pl.ds(k0, span), :], preferred_element_type=F32)
            out = (r[:, :HEAD_DIM] / (r[:, HEAD_DIM:] + e_ref[...])).astype(BF16)
            for g in range(Q_PER_KV):
                col0 = j * gw + g * HEAD_DIM
                o_ref[pl.ds(q0, BLOCK), col0:col0 + HEAD_DIM] = out[g * BLOCK:(g + 1) * BLOCK]

        def pair(i, n_logits, n_softmax):
            for half in range(2):
                blk = i + half
                if half < n_logits:
                    logits(blk + 2, s_refs[half])
                if half < n_softmax:
                    softmax(s_refs[1 - half], p_refs[1 - half], e_refs[1 - half])
                values(blk, p_refs[half], e_refs[half])

        logits(0, s_refs[0])
        logits(1, s_refs[1])
        softmax(s_refs[0], p_refs[0], e_refs[0])

        def body(it, carry):
            pair(4 * it, 2, 2)
            pair(4 * it + 2, 2, 2)
            return carry

        full_pairs = n_blocks // 2 - 1
        lax.fori_loop(0, full_pairs // 2, body, 0)
        if full_pairs % 2:
            pair(n_blocks - 4, 2, 2)
        pair(n_blocks - 2, 0, 1)

    for j in range(heads_per_step):
        one_kv_head(j)


def _attention(proj, slopes, sink, batch, seq, *, attn_width, kv_width, heads_per_step=2):
    m = proj.shape[0]
    n_kv = kv_width // HEAD_DIM
    assert n_kv % heads_per_step == 0
    qw = heads_per_step * Q_PER_KV * HEAD_DIM
    kw = heads_per_step * HEAD_DIM
    assert attn_width % kw == 0 and kv_width % kw == 0
    k_blk0 = attn_width // kw
    v_blk0 = (attn_width + kv_width) // kw
    rows = Q_PER_KV * BLOCK
    grid_spec = pltpu.PrefetchScalarGridSpec(
        num_scalar_prefetch=2,
        grid=(batch, n_kv // heads_per_step),
        in_specs=[pl.BlockSpec((seq, qw), lambda b, h, *_: (b, h)),
                  pl.BlockSpec((seq, kw), lambda b, h, *_: (b, k_blk0 + h)),
                  pl.BlockSpec((seq, kw), lambda b, h, *_: (b, v_blk0 + h))],
        out_specs=pl.BlockSpec((seq, qw), lambda b, h, *_: (b, h)),
        scratch_shapes=[pltpu.VMEM((n_kv, 5, rows, BLOCK), F32)] + heads_per_step * [
            pltpu.VMEM((seq, 2 * HEAD_DIM), BF16),
            *[pltpu.VMEM((rows, 3 * BLOCK), F32) for _ in range(2)],
            *[pltpu.VMEM((rows, 3 * BLOCK), BF16) for _ in range(2)],
            *[pltpu.VMEM((rows, HEAD_DIM), F32) for _ in range(2)]])
    return pl.pallas_call(
        functools.partial(_attn_kernel, seq=seq, heads_per_step=heads_per_step),
        out_shape=jax.ShapeDtypeStruct((m, attn_width), BF16),
        grid_spec=grid_spec,
        name="attention",
        compiler_params=_params("arbitrary", "arbitrary"),
    )(slopes, sink, proj, proj, proj)


def _fourier_kernel(u_ref, cd_ref, sd_ref, ch_ref, sh_ref, pm_ref, o_ref,
                    ue_ref, uo_ref, ae_ref, bo_ref, y_ref, *, seq, gd):
    half = seq // 2
    n_blocks = half // BLOCK
    pm = pm_ref[...]

    def reversed_block(src_ref, b, end):
        lo = end - (b + 1) * BLOCK
        if b == 0:
            return jnp.dot(pm[:, :BLOCK], src_ref[lo:end, :], preferred_element_type=F32)
        return jnp.dot(pm, src_ref[lo:lo + 2 * BLOCK, :], preferred_element_type=F32)

    for b in range(n_blocks):
        rows = slice(b * BLOCK, (b + 1) * BLOCK)
        x = u_ref[rows, :].astype(F32)
        r = reversed_block(u_ref, b, seq)
        ue_ref[rows, :] = (x + r).astype(BF16)
        uo_ref[rows, :] = (x - r).astype(BF16)

    mid = []
    for g in range(N_FOURIER_GROUPS):
        cols = slice(g * gd, (g + 1) * gd)
        ae_ref[:, cols] = jnp.dot(ue_ref[:, cols], cd_ref[...],
                                  preferred_element_type=F32).astype(BF16)
        bo_ref[:, cols] = jnp.dot(uo_ref[:, cols], sd_ref[...],
                                  preferred_element_type=F32).astype(BF16)
        mid.append(jnp.dot(u_ref[half:half + BF16_SUBLANE_TILE, cols], cd_ref[...],
                           preferred_element_type=F32)[0:1])
    a_mid = jnp.concatenate(mid, axis=1) * (seq ** -0.5)

    p = jnp.dot(ch_ref[...], ae_ref[...], preferred_element_type=F32)
    q = jnp.dot(sh_ref[...], bo_ref[...], preferred_element_type=F32)
    k_idx = lax.broadcasted_iota(jnp.int32, (half, 1), 0)
    sign = 1.0 - 2.0 * (k_idx & 1).astype(F32)
    base = p[:half] + sign * a_mid
    o_ref[0:half, :] = (base - q).astype(BF16)
    y_ref[...] = (base + q).astype(BF16)
    z_mid = p[half:half + 1] + a_mid

    for b in range(n_blocks):
        r = reversed_block(y_ref, b, half)
        if b == 0:
            first_row = lax.broadcasted_iota(jnp.int32, (BLOCK, 1), 0) == 0
            r = jnp.where(first_row, z_mid, r)
        o_ref[half + b * BLOCK:half + (b + 1) * BLOCK, :] = r.astype(BF16)


def _dft_tables(seq, gd):
    half = seq // 2

    def angles(rows, cols, n):
        k = jnp.arange(rows, dtype=jnp.int32)[:, None]
        s = jnp.arange(cols, dtype=jnp.int32)[None, :]
        return ((k * s) % n).astype(F32) * (2.0 * jnp.pi / n)

    ang_c = angles(gd, gd, gd)
    cd = jnp.cos(ang_c) * (gd ** -0.5)
    sd = jnp.sin(ang_c) * (gd ** -0.5)
    pad_rows = half + BF16_SUBLANE_TILE
    fine = 32
    k = jnp.arange(pad_rows, dtype=jnp.int32)[:, None]
    s_hi = fine * jnp.arange(half // fine, dtype=jnp.int32)[None, :]
    ang_hi = (((k * s_hi) % seq).astype(F32) * (2.0 * jnp.pi / seq))[:, :, None]
    ang_lo = angles(pad_rows, fine, seq)[:, None, :]
    cos_p = (jnp.cos(ang_hi) * jnp.cos(ang_lo) - jnp.sin(ang_hi) * jnp.sin(ang_lo))
    sin_p = (jnp.sin(ang_hi) * jnp.cos(ang_lo) + jnp.cos(ang_hi) * jnp.sin(ang_lo))
    live = jnp.arange(pad_rows)[:, None] <= half
    ch = jnp.where(live, cos_p.reshape(pad_rows, half), 0.0) * (seq ** -0.5)
    sh = sin_p.reshape(pad_rows, half)[:half] * (seq ** -0.5)
    a = jnp.arange(BLOCK)[:, None]
    j = jnp.arange(2 * BLOCK)[None, :]
    pm = jnp.where(a == 0, j == BLOCK, j == BLOCK - a)
    return tuple(t.astype(BF16) for t in (cd, sd, ch, sh, pm))


def _fourier(proj, tables, batch, seq, *, u_col0, f_width):
    m = proj.shape[0]
    gd = f_width // N_FOURIER_GROUPS
    half = seq // 2
    cd, sd, ch, sh, pm = tables
    const = lambda shape: _resident(shape, lambda b: (0, 0))
    return pl.pallas_call(
        functools.partial(_fourier_kernel, seq=seq, gd=gd),
        out_shape=jax.ShapeDtypeStruct((m, f_width), BF16),
        grid=(batch,),
        in_specs=[pl.BlockSpec((seq, f_width), lambda b: (b, u_col0 // f_width)),
                  const(cd.shape), const(sd.shape), const(ch.shape), const(sh.shape),
                  const(pm.shape)],
        out_specs=pl.BlockSpec((seq, f_width), lambda b: (b, 0)),
        scratch_shapes=[pltpu.VMEM((half, f_width), BF16) for _ in range(5)],
        name="fourier",
        compiler_params=_params("parallel"),
    )(proj, cd, sd, ch, sh, pm)


def _merge_kernel(a_ref, f_ref, ga_ref, gf_ref, wa_ref, wf_ref, o_ref):
    ta = jnp.dot(a_ref[...], wa_ref[...], preferred_element_type=F32)
    tf = jnp.dot(f_ref[...], wf_ref[...], preferred_element_type=F32)
    o_ref[...] = (ga_ref[...].astype(F32) * ta + gf_ref[...].astype(F32) * tf).astype(BF16)


def _merge(attn, four, projg, wa, wf, *, gate_col0, tm=1024, tn=1024):
    m, aw = attn.shape
    fw = four.shape[1]
    d = wa.shape[1]
    ga0 = gate_col0 // tn
    gf0 = (gate_col0 + d) // tn
    return pl.pallas_call(
        _merge_kernel,
        out_shape=jax.ShapeDtypeStruct((m, d), BF16),
        grid=(m // tm, d // tn),
        in_specs=[pl.BlockSpec((tm, aw), lambda i, j: (i, 0)),
                  pl.BlockSpec((tm, fw), lambda i, j: (i, 0)),
                  pl.BlockSpec((tm, tn), lambda i, j: (i, ga0 + j)),
                  pl.BlockSpec((tm, tn), lambda i, j: (i, gf0 + j)),
                  pl.BlockSpec((aw, tn), lambda i, j: (0, j)),
                  pl.BlockSpec((fw, tn), lambda i, j: (0, j))],
        out_specs=pl.BlockSpec((tm, tn), lambda i, j: (i, j)),
        name="merge",
        compiler_params=_params("parallel", "arbitrary"),
    )(attn, four, projg, projg, wa, wf)


def _out_kernel(m_ref, x_ref, mod_ref, g_ref, wo_ref, x1_ref, h2_ref):
    out = jnp.dot(m_ref[...], wo_ref[...], preferred_element_type=F32)
    x1 = x_ref[...] + mod_ref[0, 2:3, :] * out
    x1_ref[...] = x1
    h2 = _rms_modulate(x1, g_ref[...], mod_ref[0, 4:5, :], mod_ref[0, 3:4, :])
    h2_ref[...] = h2.astype(BF16)


def _out_proj(merged, x2d, mod, b_off, seq, g_ffn, wo, *, tm=512):
    m, d = x2d.shape
    tiles_per_seq = seq // tm
    return pl.pallas_call(
        _out_kernel,
        out_shape=(jax.ShapeDtypeStruct((m, d), F32), jax.ShapeDtypeStruct((m, d), BF16)),
        grid=(m // tm,),
        in_specs=[pl.BlockSpec((tm, d), lambda i: (i, 0)),
                  pl.BlockSpec((tm, d), lambda i: (i, 0)),
                  pl.BlockSpec((1, N_MOD, d), lambda i: (b_off + i // tiles_per_seq, 0, 0)),
                  pl.BlockSpec((1, d), lambda i: (0, 0)),
                  _resident((d, d), lambda i: (0, 0))],
        out_specs=(pl.BlockSpec((tm, d), lambda i: (i, 0)),
                   pl.BlockSpec((tm, d), lambda i: (i, 0))),
        name="out_proj",
        compiler_params=_params("parallel"),
    )(merged, x2d, mod, g_ffn.reshape(1, d), wo)


def _ffn_up_kernel(h_ref, wg_ref, wu_ref, o_ref):
    h = h_ref[...]
    gt = jnp.dot(h, wg_ref[...], preferred_element_type=F32)
    up = jnp.dot(h, wu_ref[...], preferred_element_type=F32)
    o_ref[...] = (gt * jax.nn.sigmoid(gt) * up).astype(BF16)


def _ffn_up(h2, w_up, *, tm=512, tn=2816):
    m, d = h2.shape
    dff = w_up.shape[1] // 2
    nj = dff // tn
    return pl.pallas_call(
        _ffn_up_kernel,
        out_shape=jax.ShapeDtypeStruct((m, dff), BF16),
        grid=(nj, m // tm),
        in_specs=[pl.BlockSpec((tm, d), lambda j, i: (i, 0)),
                  _resident((d, tn), lambda j, i: (0, j)),
                  _resident((d, tn), lambda j, i: (0, nj + j))],
        out_specs=pl.BlockSpec((tm, tn), lambda j, i: (i, j)),
        name="ffn_up",
        compiler_params=_params("arbitrary", "parallel"),
    )(h2, w_up, w_up)


def _ffn_down_kernel(a_ref, x1_ref, mod_ref, g_ref, wd_ref, y_ref, *, final_norm):
    out = jnp.dot(a_ref[...], wd_ref[...], preferred_element_type=F32)
    x2 = x1_ref[...] + mod_ref[0, 5:6, :] * out
    if final_norm:
        y = x2 * lax.rsqrt(jnp.mean(x2 * x2, axis=-1, keepdims=True) + RMS_EPS)
        x2 = y * g_ref[...]
    y_ref[...] = x2


def _ffn_down(act, x1, mod, b_off, seq, g_final, wd, *, final_norm, tm=512):
    m, d = x1.shape
    dff = act.shape[1]
    tiles_per_seq = seq // tm
    return pl.pallas_call(
        functools.partial(_ffn_down_kernel, final_norm=final_norm),
        out_shape=jax.ShapeDtypeStruct((m, d), F32),
        grid=(m // tm,),
        in_specs=[pl.BlockSpec((tm, dff), lambda i: (i, 0)),
                  pl.BlockSpec((tm, d), lambda i: (i, 0)),
                  pl.BlockSpec((1, N_MOD, d), lambda i: (b_off + i // tiles_per_seq, 0, 0)),
                  pl.BlockSpec((1, d), lambda i: (0, 0)),
                  _resident((dff, d), lambda i: (0, 0))],
        out_specs=pl.BlockSpec((tm, d), lambda i: (i, 0)),
        name="ffn_down",
        compiler_params=_params("parallel"),
    )(act, x1, mod, g_final.reshape(1, d), wd)


def kernel(x_prompt, x_sample, c_prompt, c_sample, w_mod, b_mod, g_mix, w_in, attn_sink,
           w_attn_branch, w_fourier_branch, w_gate, b_gate, w_out, g_ffn, w_up, w_down,
           g_final):
    depth, d, in_width = w_in.shape
    attn_width = w_attn_branch.shape[1]
    f_width = w_fourier_branch.shape[1]
    kv_width = (in_width - attn_width - f_width) // 2
    n_heads = attn_width // HEAD_DIM
    groups = [(x_prompt, 0), (x_sample, c_prompt.shape[0])]
    seq = x_prompt.shape[1]
    assert x_sample.shape[1] == seq and seq % (4 * BLOCK) == 0
    assert (attn_width + 2 * kv_width) % f_width == 0
    assert w_gate.shape[2] == in_width

    c_all = jnp.concatenate([c_prompt, c_sample], axis=0)
    pad = -c_all.shape[0] % BF16_SUBLANE_TILE
    c_all = jnp.pad(c_all, ((0, pad), (0, 0)))

    heads = jnp.arange(1, n_heads + 1, dtype=F32)
    slopes = jnp.exp2(-8.0 * heads / n_heads)
    dft_tables = _dft_tables(seq, f_width // N_FOURIER_GROUPS)

    xs = [x.reshape(-1, d) for x, _ in groups]
    for l in range(depth):
        last = l == depth - 1
        mod = _modulation(c_all, w_mod[l], b_mod[l]).reshape(-1, N_MOD, d)
        w_pair = jnp.stack([w_in[l], w_gate[l]]).astype(BF16)
        wa = w_attn_branch[l].astype(BF16)
        wf = w_fourier_branch[l].astype(BF16)
        wo = w_out[l].astype(BF16)
        wu = w_up[l].astype(BF16)
        wd = w_down[l].astype(BF16)
        sink = attn_sink[l].astype(F32)
        new_xs = []
        for x2d, (x_in, b_off) in zip(xs, groups):
            batch = x_in.shape[0]
            projg = _projection(x2d, mod, b_off, seq, g_mix[l], w_pair, b_gate[l])
            attn = _attention(projg, slopes, sink, batch, seq,
                              attn_width=attn_width, kv_width=kv_width)
            four = _fourier(projg, dft_tables, batch, seq,
                            u_col0=attn_width + 2 * kv_width, f_width=f_width)
            merged = _merge(attn, four, projg, wa, wf, gate_col0=in_width)
            x1, h2 = _out_proj(merged, x2d, mod, b_off, seq, g_ffn[l], wo)
            act = _ffn_up(h2, wu)
            new_xs.append(_ffn_down(act, x1, mod, b_off, seq, g_final, wd, final_norm=last))
        xs = new_xs
    return tuple(x2d.reshape(x_in.shape) for x2d, (x_in, _) in zip(xs, groups))
```

```python
import functools
import math

import jax
import jax.numpy as jnp
from jax import lax
from jax.experimental import pallas as pl
from jax.experimental.pallas import tpu as pltpu

F32 = jnp.float32
BF16 = jnp.bfloat16

HEAD_DIM = 128
Q_PER_KV = 4
N_FOURIER_GROUPS = 4
WINDOW = 128
BLOCK = 128
N_MOD = 6
RMS_EPS = 1e-6
LOG2E = math.log2(math.e)

V7X_VMEM_LIMIT_BYTES = 60 * 1024 * 1024
BF16_SUBLANE_TILE = 16


def _params(*semantics):
    return pltpu.CompilerParams(dimension_semantics=semantics,
                                vmem_limit_bytes=V7X_VMEM_LIMIT_BYTES)


def _resident(block_shape, index_map):
    return pl.BlockSpec(block_shape, index_map, pipeline_mode=pl.Buffered(1))


def _rms_modulate(x, gain, scale, shift):
    y = x * lax.rsqrt(jnp.mean(x * x, axis=-1, keepdims=True) + RMS_EPS)
    return y * (gain * (1.0 + scale)) + shift


def _mod_kernel(c_ref, w_ref, b_ref, o_ref):
    c = c_ref[...]
    a = (c * jax.nn.sigmoid(c)).astype(BF16)
    o_ref[...] = jnp.dot(a, w_ref[...].astype(BF16), preferred_element_type=F32) + b_ref[...]


def _modulation(c, w_mod, b_mod, *, tn=512):
    bp, d = c.shape
    n = w_mod.shape[1]
    return pl.pallas_call(
        _mod_kernel,
        out_shape=jax.ShapeDtypeStruct((bp, n), F32),
        grid=(n // tn,),
        in_specs=[pl.BlockSpec((bp, d), lambda j: (0, 0)),
                  pl.BlockSpec((d, tn), lambda j: (0, j)),
                  pl.BlockSpec((1, tn), lambda j: (0, j))],
        out_specs=pl.BlockSpec((bp, tn), lambda j: (0, j)),
        name="modulation",
        compiler_params=_params("arbitrary"),
    )(c, w_mod, b_mod.reshape(1, n))


def _proj_kernel(x0_ref, xn_ref, mod0_ref, modn_ref, g_ref, w_ref, b_ref, o_ref,
                 h_even_ref, h_odd_ref, *, n_in_tiles, n_col_tiles, row_chunk):
    i = pl.program_id(0)
    j = pl.program_id(1)
    odd = lax.rem(i, 2) == 1
    gain = g_ref[...]
    tm = xn_ref.shape[0]
    rows_per_step = tm // n_col_tiles

    def norm_rows(x_ref, mod_ref, dst_ref, row0, n_rows):
        rows = pl.ds(pl.multiple_of(row0, n_rows), n_rows)
        h = _rms_modulate(x_ref[rows, :], gain, mod_ref[0, 1:2, :], mod_ref[0, 0:1, :])
        dst_ref[rows, :] = h.astype(BF16)

    @pl.when((i == 0) & (j == 0))
    def _():
        def body(r, carry):
            norm_rows(x0_ref, mod0_ref, h_even_ref, r * row_chunk, row_chunk)
            return carry

        lax.fori_loop(0, tm // row_chunk, body, 0)

    def column_step(cur_ref, nxt_ref, gated):
        norm_rows(xn_ref, modn_ref, nxt_ref, j * rows_per_step, rows_per_step)
        acc = jnp.dot(cur_ref[...], w_ref[...], preferred_element_type=F32)
        if gated:
            acc = 0.5 * jnp.tanh(0.5 * (acc + b_ref[...])) + 0.5
        o_ref[...] = acc.astype(BF16)

    for is_odd, cur_ref, nxt_ref in ((False, h_even_ref, h_odd_ref),
                                     (True, h_odd_ref, h_even_ref)):
        for gated in (False, True):
            pl.when((odd == is_odd) & ((j >= n_in_tiles) == gated))(
                functools.partial(column_step, cur_ref, nxt_ref, gated))


def _projection(x2d, mod, b_off, seq, g_mix, w_pair, b_gate, *, tm=1024, tn=1024):
    m, d = x2d.shape
    n = 2 * w_pair.shape[2]
    n_in_tiles = w_pair.shape[2] // tn
    n_col_tiles = n // tn
    tiles_per_seq = seq // tm
    last = m // tm - 1
    kern = functools.partial(_proj_kernel, n_in_tiles=n_in_tiles, n_col_tiles=n_col_tiles,
                             row_chunk=256)
    nxt = lambda i: jnp.minimum(i + 1, last)
    return pl.pallas_call(
        kern,
        out_shape=jax.ShapeDtypeStruct((m, n), BF16),
        grid=(m // tm, n_col_tiles),
        in_specs=[_resident((tm, d), lambda i, j: (0, 0)),
                  pl.BlockSpec((tm, d), lambda i, j: (nxt(i), 0)),
                  _resident((1, N_MOD, d), lambda i, j: (b_off, 0, 0)),
                  pl.BlockSpec((1, N_MOD, d),
                               lambda i, j: (b_off + nxt(i) // tiles_per_seq, 0, 0)),
                  pl.BlockSpec((1, d), lambda i, j: (0, 0)),
                  pl.BlockSpec((None, d, tn),
                               lambda i, j: (j // n_in_tiles, 0, lax.rem(j, n_in_tiles))),
                  pl.BlockSpec((1, tn), lambda i, j: (0, jnp.maximum(j - n_in_tiles, 0)))],
        out_specs=pl.BlockSpec((tm, tn), lambda i, j: (i, j)),
        scratch_shapes=[pltpu.VMEM((tm, d), BF16), pltpu.VMEM((tm, d), BF16)],
        name="projection",
        compiler_params=_params("arbitrary", "arbitrary"),
    )(x2d, x2d, mod, mod, g_mix.reshape(1, d), w_pair, b_gate.reshape(1, -1))


def _attn_kernel(slopes_ref, sink_ref, q_ref, k_ref, v_ref, o_ref,
                 bias_ref, v1_ref, kt_ref, s0_ref, s1_ref, p0_ref, p1_ref, e0_ref, e1_ref, *,
                 seq):
    s_refs, p_refs, e_refs = (s0_ref, s1_ref), (p0_ref, p1_ref), (e0_ref, e1_ref)
    kh = pl.program_id(1)
    n_chunks = 3
    span = n_chunks * BLOCK
    n_blocks = seq // BLOCK
    logit_scale = (HEAD_DIM ** -0.5) * LOG2E

    @pl.when(pl.program_id(0) == 0)
    def _():
        q_pos = lax.broadcasted_iota(jnp.int32, (BLOCK, BLOCK), 0)
        k_pos = lax.broadcasted_iota(jnp.int32, (BLOCK, BLOCK), 1)
        for d in range(-2, 3):
            rel = d * BLOCK + k_pos - q_pos
            valid = jnp.abs(rel) <= WINDOW
            dist = jnp.abs(rel).astype(F32)
            for g in range(Q_PER_KV):
                slope2 = slopes_ref[kh * Q_PER_KV + g] * LOG2E
                bias_ref[kh, d + 2, g * BLOCK:(g + 1) * BLOCK, :] = jnp.where(
                    valid, -slope2 * dist, -jnp.inf)
        v1_ref[:, HEAD_DIM:] = jnp.ones((seq, HEAD_DIM), BF16)

    sinks2 = [sink_ref[kh * Q_PER_KV + g] * LOG2E for g in range(Q_PER_KV)]
    v1_ref[:, :HEAD_DIM] = v_ref[...]
    for c in range(n_blocks):
        kt_ref[c] = k_ref[c * BLOCK:(c + 1) * BLOCK, :].T

    def window(i):
        q0 = pl.multiple_of(i * BLOCK, BLOCK)
        k0 = pl.multiple_of(jnp.clip(q0 - BLOCK, 0, seq - span), BLOCK)
        return q0, k0

    def logits(i, s_ref):
        q0, k0 = window(i)
        first = lax.div(k0 - q0 + 2 * BLOCK, BLOCK)
        q = q_ref[pl.ds(q0, BLOCK), :]
        qs = jnp.concatenate([q[:, g * HEAD_DIM:(g + 1) * HEAD_DIM] for g in range(Q_PER_KV)],
                             axis=0)
        c0 = lax.div(k0, BLOCK)
        kw_t = jnp.concatenate([kt_ref[c0 + t] for t in range(n_chunks)], axis=1)
        s = jnp.dot(qs, kw_t, preferred_element_type=F32)
        for t in range(n_chunks):
            cols = slice(t * BLOCK, (t + 1) * BLOCK)
            s_ref[:, cols] = s[:, cols] * logit_scale + bias_ref[kh, first + t]

    def softmax(s_ref, p_ref, e_ref):
        for g in range(Q_PER_KV):
            rows = slice(g * BLOCK, (g + 1) * BLOCK)
            a = s_ref[rows, :]
            mx = jnp.maximum(jnp.max(a, axis=-1, keepdims=True), sinks2[g])
            p_ref[rows, :] = jnp.exp2(a - mx).astype(BF16)
            e_ref[rows, :] = jnp.broadcast_to(jnp.exp2(sinks2[g] - mx), (BLOCK, HEAD_DIM))

    def values(i, p_ref, e_ref):
        q0, k0 = window(i)
        r = jnp.dot(p_ref[...], v1_ref[pl.ds(k0, span), :], preferred_element_type=F32)
        out = (r[:, :HEAD_DIM] / (r[:, HEAD_DIM:] + e_ref[...])).astype(BF16)
        for g in range(Q_PER_KV):
            o_ref[pl.ds(q0, BLOCK), g * HEAD_DIM:(g + 1) * HEAD_DIM] = (
                out[g * BLOCK:(g + 1) * BLOCK])

    def pair(i, n_logits, n_softmax):
        for half in range(2):
            blk = i + half
            if half < n_logits:
                logits(blk + 2, s_refs[half])
            if half < n_softmax:
                softmax(s_refs[1 - half], p_refs[1 - half], e_refs[1 - half])
            values(blk, p_refs[half], e_refs[half])

    logits(0, s_refs[0])
    logits(1, s_refs[1])
    softmax(s_refs[0], p_refs[0], e_refs[0])

    def body(it, carry):
        pair(4 * it, 2, 2)
        pair(4 * it + 2, 2, 2)
        return carry

    full_pairs = n_blocks // 2 - 1
    lax.fori_loop(0, full_pairs // 2, body, 0)
    if full_pairs % 2:
        pair(n_blocks - 4, 2, 2)
    pair(n_blocks - 2, 0, 1)


def _attention(proj, slopes, sink, batch, seq, *, attn_width, kv_width):
    m = proj.shape[0]
    n_kv = kv_width // HEAD_DIM
    gw = Q_PER_KV * HEAD_DIM
    k_col0 = attn_width // HEAD_DIM
    v_col0 = (attn_width + kv_width) // HEAD_DIM
    smem = pl.BlockSpec(memory_space=pltpu.SMEM)
    rows = Q_PER_KV * BLOCK
    return pl.pallas_call(
        functools.partial(_attn_kernel, seq=seq),
        out_shape=jax.ShapeDtypeStruct((m, attn_width), BF16),
        grid=(batch, n_kv),
        in_specs=[smem, smem,
                  pl.BlockSpec((seq, gw), lambda b, h: (b, h)),
                  pl.BlockSpec((seq, HEAD_DIM), lambda b, h: (b, k_col0 + h)),
                  pl.BlockSpec((seq, HEAD_DIM), lambda b, h: (b, v_col0 + h))],
        out_specs=pl.BlockSpec((seq, gw), lambda b, h: (b, h)),
        scratch_shapes=[pltpu.VMEM((n_kv, 5, rows, BLOCK), F32),
                        pltpu.VMEM((seq, 2 * HEAD_DIM), BF16),
                        pltpu.VMEM((seq // BLOCK, HEAD_DIM, BLOCK), BF16),
                        *[pltpu.VMEM((rows, 3 * BLOCK), F32) for _ in range(2)],
                        *[pltpu.VMEM((rows, 3 * BLOCK), BF16) for _ in range(2)],
                        *[pltpu.VMEM((rows, HEAD_DIM), F32) for _ in range(2)]],
        name="attention",
        compiler_params=_params("arbitrary", "arbitrary"),
    )(slopes, sink, proj, proj, proj)


def _fourier_kernel(u_ref, cd_ref, sd_ref, ch_ref, sh_ref, pm_ref, o_ref,
                    ue_ref, uo_ref, ae_ref, bo_ref, y_ref, *, seq, gd):
    half = seq // 2
    n_blocks = half // BLOCK
    pm = pm_ref[...]

    def reversed_block(src_ref, b, end):
        lo = end - (b + 1) * BLOCK
        if b == 0:
            return jnp.dot(pm[:, :BLOCK], src_ref[lo:end, :], preferred_element_type=F32)
        return jnp.dot(pm, src_ref[lo:lo + 2 * BLOCK, :], preferred_element_type=F32)

    for b in range(n_blocks):
        rows = slice(b * BLOCK, (b + 1) * BLOCK)
        x = u_ref[rows, :].astype(F32)
        r = reversed_block(u_ref, b, seq)
        ue_ref[rows, :] = (x + r).astype(BF16)
        uo_ref[rows, :] = (x - r).astype(BF16)

    mid = []
    for g in range(N_FOURIER_GROUPS):
        cols = slice(g * gd, (g + 1) * gd)
        ae_ref[:, cols] = jnp.dot(ue_ref[:, cols], cd_ref[...],
                                  preferred_element_type=F32).astype(BF16)
        bo_ref[:, cols] = jnp.dot(uo_ref[:, cols], sd_ref[...],
                                  preferred_element_type=F32).astype(BF16)
        mid.append(jnp.dot(u_ref[half:half + BF16_SUBLANE_TILE, cols], cd_ref[...],
                           preferred_element_type=F32)[0:1])
    a_mid = jnp.concatenate(mid, axis=1) * (seq ** -0.5)

    p = jnp.dot(ch_ref[...], ae_ref[...], preferred_element_type=F32)
    q = jnp.dot(sh_ref[...], bo_ref[...], preferred_element_type=F32)
    k_idx = lax.broadcasted_iota(jnp.int32, (half, 1), 0)
    sign = 1.0 - 2.0 * (k_idx & 1).astype(F32)
    base = p[:half] + sign * a_mid
    o_ref[0:half, :] = (base - q).astype(BF16)
    y_ref[...] = (base + q).astype(BF16)
    z_mid = p[half:half + 1] + a_mid

    for b in range(n_blocks):
        r = reversed_block(y_ref, b, half)
        if b == 0:
            first_row = lax.broadcasted_iota(jnp.int32, (BLOCK, 1), 0) == 0
            r = jnp.where(first_row, z_mid, r)
        o_ref[half + b * BLOCK:half + (b + 1) * BLOCK, :] = r.astype(BF16)


def _dft_tables(seq, gd):
    half = seq // 2

    def angles(rows, cols, n):
        k = jnp.arange(rows, dtype=jnp.int32)[:, None]
        s = jnp.arange(cols, dtype=jnp.int32)[None, :]
        return ((k * s) % n).astype(F32) * (2.0 * jnp.pi / n)

    ang_c = angles(gd, gd, gd)
    cd = jnp.cos(ang_c) * (gd ** -0.5)
    sd = jnp.sin(ang_c) * (gd ** -0.5)
    pad_rows = half + BF16_SUBLANE_TILE
    fine = 32
    k = jnp.arange(pad_rows, dtype=jnp.int32)[:, None]
    s_hi = fine * jnp.arange(half // fine, dtype=jnp.int32)[None, :]
    ang_hi = (((k * s_hi) % seq).astype(F32) * (2.0 * jnp.pi / seq))[:, :, None]
    ang_lo = angles(pad_rows, fine, seq)[:, None, :]
    cos_p = (jnp.cos(ang_hi) * jnp.cos(ang_lo) - jnp.sin(ang_hi) * jnp.sin(ang_lo))
    sin_p = (jnp.sin(ang_hi) * jnp.cos(ang_lo) + jnp.cos(ang_hi) * jnp.sin(ang_lo))
    live = jnp.arange(pad_rows)[:, None] <= half
    ch = jnp.where(live, cos_p.reshape(pad_rows, half), 0.0) * (seq ** -0.5)
    sh = sin_p.reshape(pad_rows, half)[:half] * (seq ** -0.5)
    a = jnp.arange(BLOCK)[:, None]
    j = jnp.arange(2 * BLOCK)[None, :]
    pm = jnp.where(a == 0, j == BLOCK, j == BLOCK - a)
    return tuple(t.astype(BF16) for t in (cd, sd, ch, sh, pm))


def _fourier(proj, tables, batch, seq, *, u_col0, f_width):
    m = proj.shape[0]
    gd = f_width // N_FOURIER_GROUPS
    half = seq // 2
    cd, sd, ch, sh, pm = tables
    const = lambda shape: _resident(shape, lambda b: (0, 0))
    return pl.pallas_call(
        functools.partial(_fourier_kernel, seq=seq, gd=gd),
        out_shape=jax.ShapeDtypeStruct((m, f_width), BF16),
        grid=(batch,),
        in_specs=[pl.BlockSpec((seq, f_width), lambda b: (b, u_col0 // f_width)),
                  const(cd.shape), const(sd.shape), const(ch.shape), const(sh.shape),
                  const(pm.shape)],
        out_specs=pl.BlockSpec((seq, f_width), lambda b: (b, 0)),
        scratch_shapes=[pltpu.VMEM((half, f_width), BF16) for _ in range(5)],
        name="fourier",
        compiler_params=_params("parallel"),
    )(proj, cd, sd, ch, sh, pm)


def _merge_kernel(a_ref, f_ref, ga_ref, gf_ref, wa_ref, wf_ref, o_ref):
    ta = jnp.dot(a_ref[...], wa_ref[...], preferred_element_type=F32)
    tf = jnp.dot(f_ref[...], wf_ref[...], preferred_element_type=F32)
    o_ref[...] = (ga_ref[...].astype(F32) * ta + gf_ref[...].astype(F32) * tf).astype(BF16)


def _merge(attn, four, projg, wa, wf, *, gate_col0, tm=1024, tn=1024):
    m, aw = attn.shape
    fw = four.shape[1]
    d = wa.shape[1]
    ga0 = gate_col0 // tn
    gf0 = (gate_col0 + d) // tn
    return pl.pallas_call(
        _merge_kernel,
        out_shape=jax.ShapeDtypeStruct((m, d), BF16),
        grid=(m // tm, d // tn),
        in_specs=[pl.BlockSpec((tm, aw), lambda i, j: (i, 0)),
                  pl.BlockSpec((tm, fw), lambda i, j: (i, 0)),
                  pl.BlockSpec((tm, tn), lambda i, j: (i, ga0 + j)),
                  pl.BlockSpec((tm, tn), lambda i, j: (i, gf0 + j)),
                  pl.BlockSpec((aw, tn), lambda i, j: (0, j)),
                  pl.BlockSpec((fw, tn), lambda i, j: (0, j))],
        out_specs=pl.BlockSpec((tm, tn), lambda i, j: (i, j)),
        name="merge",
        compiler_params=_params("parallel", "arbitrary"),
    )(attn, four, projg, projg, wa, wf)


def _out_kernel(m_ref, x_ref, mod_ref, g_ref, wo_ref, x1_ref, h2_ref):
    out = jnp.dot(m_ref[...], wo_ref[...], preferred_element_type=F32)
    x1 = x_ref[...] + mod_ref[0, 2:3, :] * out
    x1_ref[...] = x1
    h2 = _rms_modulate(x1, g_ref[...], mod_ref[0, 4:5, :], mod_ref[0, 3:4, :])
    h2_ref[...] = h2.astype(BF16)


def _out_proj(merged, x2d, mod, b_off, seq, g_ffn, wo, *, tm=512):
    m, d = x2d.shape
    tiles_per_seq = seq // tm
    return pl.pallas_call(
        _out_kernel,
        out_shape=(jax.ShapeDtypeStruct((m, d), F32), jax.ShapeDtypeStruct((m, d), BF16)),
        grid=(m // tm,),
        in_specs=[pl.BlockSpec((tm, d), lambda i: (i, 0)),
                  pl.BlockSpec((tm, d), lambda i: (i, 0)),
                  pl.BlockSpec((1, N_MOD, d), lambda i: (b_off + i // tiles_per_seq, 0, 0)),
                  pl.BlockSpec((1, d), lambda i: (0, 0)),
                  _resident((d, d), lambda i: (0, 0))],
        out_specs=(pl.BlockSpec((tm, d), lambda i: (i, 0)),
                   pl.BlockSpec((tm, d), lambda i: (i, 0))),
        name="out_proj",
        compiler_params=_params("parallel"),
    )(merged, x2d, mod, g_ffn.reshape(1, d), wo)


def _ffn_up_kernel(h_ref, wg_ref, wu_ref, o_ref):
    h = h_ref[...]
    gt = jnp.dot(h, wg_ref[...], preferred_element_type=F32)
    up = jnp.dot(h, wu_ref[...], preferred_element_type=F32)
    o_ref[...] = (gt * jax.nn.sigmoid(gt) * up).astype(BF16)


def _ffn_up(h2, w_up, *, tm=512, tn=2816):
    m, d = h2.shape
    dff = w_up.shape[1] // 2
    nj = dff // tn
    return pl.pallas_call(
        _ffn_up_kernel,
        out_shape=jax.ShapeDtypeStruct((m, dff), BF16),
        grid=(nj, m // tm),
        in_specs=[pl.BlockSpec((tm, d), lambda j, i: (i, 0)),
                  _resident((d, tn), lambda j, i: (0, j)),
                  _resident((d, tn), lambda j, i: (0, nj + j))],
        out_specs=pl.BlockSpec((tm, tn), lambda j, i: (i, j)),
        name="ffn_up",
        compiler_params=_params("arbitrary", "parallel"),
    )(h2, w_up, w_up)


def _ffn_down_kernel(a_ref, x1_ref, mod_ref, g_ref, wd_ref, y_ref, *, final_norm):
    out = jnp.dot(a_ref[...], wd_ref[...], preferred_element_type=F32)
    x2 = x1_ref[...] + mod_ref[0, 5:6, :] * out
    if final_norm:
        y = x2 * lax.rsqrt(jnp.mean(x2 * x2, axis=-1, keepdims=True) + RMS_EPS)
        x2 = y * g_ref[...]
    y_ref[...] = x2


def _ffn_down(act, x1, mod, b_off, seq, g_final, wd, *, final_norm, tm=512):
    m, d = x1.shape
    dff = act.shape[1]
    tiles_per_seq = seq // tm
    return pl.pallas_call(
        functools.partial(_ffn_down_kernel, final_norm=final_norm),
        out_shape=jax.ShapeDtypeStruct((m, d), F32),
        grid=(m // tm,),
        in_specs=[pl.BlockSpec((tm, dff), lambda i: (i, 0)),
                  pl.BlockSpec((tm, d), lambda i: (i, 0)),
                  pl.BlockSpec((1, N_MOD, d), lambda i: (b_off + i // tiles_per_seq, 0, 0)),
                  pl.BlockSpec((1, d), lambda i: (0, 0)),
                  _resident((dff, d), lambda i: (0, 0))],
        out_specs=pl.BlockSpec((tm, d), lambda i: (i, 0)),
        name="ffn_down",
        compiler_params=_params("parallel"),
    )(act, x1, mod, g_final.reshape(1, d), wd)


def kernel(x_prompt, x_sample, c_prompt, c_sample, w_mod, b_mod, g_mix, w_in, attn_sink,
           w_attn_branch, w_fourier_branch, w_gate, b_gate, w_out, g_ffn, w_up, w_down,
           g_final):
    depth, d, in_width = w_in.shape
    attn_width = w_attn_branch.shape[1]
    f_width = w_fourier_branch.shape[1]
    kv_width = (in_width - attn_width - f_width) // 2
    n_heads = attn_width // HEAD_DIM
    groups = [(x_prompt, 0), (x_sample, c_prompt.shape[0])]
    seq = x_prompt.shape[1]
    assert x_sample.shape[1] == seq and seq % (4 * BLOCK) == 0
    assert (attn_width + 2 * kv_width) % f_width == 0
    assert w_gate.shape[2] == in_width

    c_all = jnp.concatenate([c_prompt, c_sample], axis=0)
    pad = -c_all.shape[0] % BF16_SUBLANE_TILE
    c_all = jnp.pad(c_all, ((0, pad), (0, 0)))

    heads = jnp.arange(1, n_heads + 1, dtype=F32)
    slopes = jnp.exp2(-8.0 * heads / n_heads)
    dft_tables = _dft_tables(seq, f_width // N_FOURIER_GROUPS)

    xs = [x.reshape(-1, d) for x, _ in groups]
    for l in range(depth):
        last = l == depth - 1
        mod = _modulation(c_all, w_mod[l], b_mod[l]).reshape(-1, N_MOD, d)
        w_pair = jnp.stack([w_in[l], w_gate[l]]).astype(BF16)
        wa = w_attn_branch[l].astype(BF16)
        wf = w_fourier_branch[l].astype(BF16)
        wo = w_out[l].astype(BF16)
        wu = w_up[l].astype(BF16)
        wd = w_down[l].astype(BF16)
        sink = attn_sink[l].astype(F32)
        new_xs = []
        for x2d, (x_in, b_off) in zip(xs, groups):
            batch = x_in.shape[0]
            projg = _projection(x2d, mod, b_off, seq, g_mix[l], w_pair, b_gate[l])
            attn = _attention(projg, slopes, sink, batch, seq,
                              attn_width=attn_width, kv_width=kv_width)
            four = _fourier(projg, dft_tables, batch, seq,
                            u_col0=attn_width + 2 * kv_width, f_width=f_width)
            merged = _merge(attn, four, projg, wa, wf, gate_col0=in_width)
            x1, h2 = _out_proj(merged, x2d, mod, b_off, seq, g_ffn[l], wo)
            act = _ffn_up(h2, wu)
            new_xs.append(_ffn_down(act, x1, mod, b_off, seq, g_final, wd, final_norm=last))
        xs = new_xs
    return tuple(x2d.reshape(x_in.shape) for x2d, (x_in, _) in zip(xs, groups))
```

```python
import functools
import math

import jax
import jax.numpy as jnp
from jax import lax
from jax.experimental import pallas as pl
from jax.experimental.pallas import tpu as pltpu

F32 = jnp.float32
BF16 = jnp.bfloat16

HEAD_DIM = 128
Q_PER_KV = 4
N_FOURIER_GROUPS = 4
WINDOW = 128
BLOCK = 128
N_MOD = 6
RMS_EPS = 1e-6
LOG2E = math.log2(math.e)

V7X_VMEM_LIMIT_BYTES = 60 * 1024 * 1024
BF16_SUBLANE_TILE = 16


def _params(*semantics):
    return pltpu.CompilerParams(dimension_semantics=semantics,
                                vmem_limit_bytes=V7X_VMEM_LIMIT_BYTES)


def _resident(block_shape, index_map):
    return pl.BlockSpec(block_shape, index_map, pipeline_mode=pl.Buffered(1))


def _rms_modulate(x, gain, scale, shift):
    y = x * lax.rsqrt(jnp.mean(x * x, axis=-1, keepdims=True) + RMS_EPS)
    return y * (gain * (1.0 + scale)) + shift


def _mod_kernel(c_ref, w_ref, b_ref, o_ref):
    c = c_ref[...]
    a = (c * jax.nn.sigmoid(c)).astype(BF16)
    o_ref[...] = jnp.dot(a, w_ref[...].astype(BF16), preferred_element_type=F32) + b_ref[...]


def _modulation(c, w_mod, b_mod, *, tn=512):
    bp, d = c.shape
    n = w_mod.shape[1]
    return pl.pallas_call(
        _mod_kernel,
        out_shape=jax.ShapeDtypeStruct((bp, n), F32),
        grid=(n // tn,),
        in_specs=[pl.BlockSpec((bp, d), lambda j: (0, 0)),
                  pl.BlockSpec((d, tn), lambda j: (0, j)),
                  pl.BlockSpec((1, tn), lambda j: (0, j))],
        out_specs=pl.BlockSpec((bp, tn), lambda j: (0, j)),
        name="modulation",
        compiler_params=_params("arbitrary"),
    )(c, w_mod, b_mod.reshape(1, n))


def _proj_kernel(x0_ref, xn_ref, mod0_ref, modn_ref, g_ref, w_ref, b_ref, o_ref,
                 h_even_ref, h_odd_ref, *, n_in_tiles, n_col_tiles, row_chunk):
    i = pl.program_id(0)
    j = pl.program_id(1)
    odd = lax.rem(i, 2) == 1
    gain = g_ref[...]
    tm = xn_ref.shape[0]
    rows_per_step = tm // n_col_tiles

    def norm_rows(x_ref, mod_ref, dst_ref, row0, n_rows):
        rows = pl.ds(pl.multiple_of(row0, n_rows), n_rows)
        h = _rms_modulate(x_ref[rows, :], gain, mod_ref[0, 1:2, :], mod_ref[0, 0:1, :])
        dst_ref[rows, :] = h.astype(BF16)
        return h

    @pl.when((i == 0) & (j == 0))
    def _():
        def body(r, carry):
            norm_rows(x0_ref, mod0_ref, h_even_ref, r * row_chunk, row_chunk)
            return carry

        lax.fori_loop(0, tm // row_chunk, body, 0)

    def column_step(cur_ref, nxt_ref, gated):
        h_next = norm_rows(xn_ref, modn_ref, nxt_ref, j * rows_per_step, rows_per_step)
        zero = jnp.where(pl.program_id(0) < 0, jnp.max(h_next), 0.0)
        tn = o_ref.shape[1]
        for lo, anchor in ((0, 0.0), (tn // 2, zero)):
            cols = slice(lo, lo + tn // 2)
            acc = jnp.dot(cur_ref[...], w_ref[:, cols], preferred_element_type=F32) + anchor
            if gated:
                half_z = (0.5 * (acc + b_ref[:, cols])).astype(BF16)
                o_ref[:, cols] = 0.5 * jnp.tanh(half_z) + 0.5
            else:
                o_ref[:, cols] = acc.astype(BF16)

    for is_odd, cur_ref, nxt_ref in ((False, h_even_ref, h_odd_ref),
                                     (True, h_odd_ref, h_even_ref)):
        for gated in (False, True):
            pl.when((odd == is_odd) & ((j >= n_in_tiles) == gated))(
                functools.partial(column_step, cur_ref, nxt_ref, gated))


def _projection(x2d, mod, b_off, seq, g_mix, w_pair, b_gate, *, tm=1024, tn=1024):
    m, d = x2d.shape
    n = 2 * w_pair.shape[2]
    n_in_tiles = w_pair.shape[2] // tn
    n_col_tiles = n // tn
    tiles_per_seq = seq // tm
    last = m // tm - 1
    kern = functools.partial(_proj_kernel, n_in_tiles=n_in_tiles, n_col_tiles=n_col_tiles,
                             row_chunk=256)
    nxt = lambda i: jnp.minimum(i + 1, last)
    return pl.pallas_call(
        kern,
        out_shape=jax.ShapeDtypeStruct((m, n), BF16),
        grid=(m // tm, n_col_tiles),
        in_specs=[_resident((tm, d), lambda i, j: (0, 0)),
                  pl.BlockSpec((tm, d), lambda i, j: (nxt(i), 0)),
                  _resident((1, N_MOD, d), lambda i, j: (b_off, 0, 0)),
                  pl.BlockSpec((1, N_MOD, d),
                               lambda i, j: (b_off + nxt(i) // tiles_per_seq, 0, 0)),
                  pl.BlockSpec((1, d), lambda i, j: (0, 0)),
                  pl.BlockSpec((None, d, tn),
                               lambda i, j: (j // n_in_tiles, 0, lax.rem(j, n_in_tiles))),
                  pl.BlockSpec((1, tn), lambda i, j: (0, jnp.maximum(j - n_in_tiles, 0)))],
        out_specs=pl.BlockSpec((tm, tn), lambda i, j: (i, j)),
        scratch_shapes=[pltpu.VMEM((tm, d), BF16), pltpu.VMEM((tm, d), BF16)],
        name="projection",
        compiler_params=_params("arbitrary", "arbitrary"),
    )(x2d, x2d, mod, mod, g_mix.reshape(1, d), w_pair, b_gate.reshape(1, -1))


def _attn_kernel(slopes_ref, sink_ref, q_ref, k_ref, v_ref, o_ref,
                 bias_ref, v1_ref, kt_ref, s0_ref, s1_ref, p0_ref, p1_ref, e0_ref, e1_ref, *,
                 seq):
    s_refs, p_refs, e_refs = (s0_ref, s1_ref), (p0_ref, p1_ref), (e0_ref, e1_ref)
    kh = pl.program_id(1)
    n_chunks = 3
    span = n_chunks * BLOCK
    n_blocks = seq // BLOCK
    logit_scale = (HEAD_DIM ** -0.5) * LOG2E

    @pl.when(pl.program_id(0) == 0)
    def _():
        q_pos = lax.broadcasted_iota(jnp.int32, (BLOCK, BLOCK), 0)
        k_pos = lax.broadcasted_iota(jnp.int32, (BLOCK, BLOCK), 1)
        for d in range(-2, 3):
            rel = d * BLOCK + k_pos - q_pos
            valid = jnp.abs(rel) <= WINDOW
            dist = jnp.abs(rel).astype(F32)
            for g in range(Q_PER_KV):
                slope2 = slopes_ref[kh * Q_PER_KV + g] * LOG2E
                bias_ref[kh, d + 2, g * BLOCK:(g + 1) * BLOCK, :] = jnp.where(
                    valid, -slope2 * dist, -jnp.inf)
        v1_ref[:, HEAD_DIM:] = jnp.ones((seq, HEAD_DIM), BF16)

    sinks2 = [sink_ref[kh * Q_PER_KV + g] * LOG2E for g in range(Q_PER_KV)]
    v1_ref[:, :HEAD_DIM] = v_ref[...]
    for c in range(n_blocks):
        kt_ref[c] = k_ref[c * BLOCK:(c + 1) * BLOCK, :].T

    def window(i):
        q0 = pl.multiple_of(i * BLOCK, BLOCK)
        k0 = pl.multiple_of(jnp.clip(q0 - BLOCK, 0, seq - span), BLOCK)
        return q0, k0

    def logits(i, s_ref):
        q0, k0 = window(i)
        first = lax.div(k0 - q0 + 2 * BLOCK, BLOCK)
        q = q_ref[pl.ds(q0, BLOCK), :]
        qs = jnp.concatenate([q[:, g * HEAD_DIM:(g + 1) * HEAD_DIM] for g in range(Q_PER_KV)],
                             axis=0)
        c0 = lax.div(k0, BLOCK)
        kw_t = jnp.concatenate([kt_ref[c0 + t] for t in range(n_chunks)], axis=1)
        s = jnp.dot(qs, kw_t, preferred_element_type=F32)
        for t in range(n_chunks):
            cols = slice(t * BLOCK, (t + 1) * BLOCK)
            s_ref[:, cols] = s[:, cols] * logit_scale + bias_ref[kh, first + t]

    def softmax(s_ref, p_ref, e_ref):
        for g in range(Q_PER_KV):
            rows = slice(g * BLOCK, (g + 1) * BLOCK)
            a = s_ref[rows, :]
            mx = jnp.maximum(jnp.max(a, axis=-1, keepdims=True), sinks2[g])
            p_ref[rows, :] = jnp.exp2(a - mx).astype(BF16)
            e_ref[rows, :] = jnp.broadcast_to(jnp.exp2(sinks2[g] - mx), (BLOCK, HEAD_DIM))

    def values(i, p_ref, e_ref):
        q0, k0 = window(i)
        r = jnp.dot(p_ref[...], v1_ref[pl.ds(k0, span), :], preferred_element_type=F32)
        out = (r[:, :HEAD_DIM] / (r[:, HEAD_DIM:] + e_ref[...])).astype(BF16)
        for g in range(Q_PER_KV):
            o_ref[pl.ds(q0, BLOCK), g * HEAD_DIM:(g + 1) * HEAD_DIM] = (
                out[g * BLOCK:(g + 1) * BLOCK])

    def pair(i, n_logits, n_softmax):
        for half in range(2):
            blk = i + half
            if half < n_logits:
                logits(blk + 2, s_refs[half])
            if half < n_softmax:
                softmax(s_refs[1 - half], p_refs[1 - half], e_refs[1 - half])
            values(blk, p_refs[half], e_refs[half])

    logits(0, s_refs[0])
    logits(1, s_refs[1])
    softmax(s_refs[0], p_refs[0], e_refs[0])

    def body(it, carry):
        pair(4 * it, 2, 2)
        pair(4 * it + 2, 2, 2)
        return carry

    full_pairs = n_blocks // 2 - 1
    lax.fori_loop(0, full_pairs // 2, body, 0)
    if full_pairs % 2:
        pair(n_blocks - 4, 2, 2)
    pair(n_blocks - 2, 0, 1)


def _attention(proj, slopes, sink, batch, seq, *, attn_width, kv_width):
    m = proj.shape[0]
    n_kv = kv_width // HEAD_DIM
    gw = Q_PER_KV * HEAD_DIM
    k_col0 = attn_width // HEAD_DIM
    v_col0 = (attn_width + kv_width) // HEAD_DIM
    smem = pl.BlockSpec(memory_space=pltpu.SMEM)
    rows = Q_PER_KV * BLOCK
    return pl.pallas_call(
        functools.partial(_attn_kernel, seq=seq),
        out_shape=jax.ShapeDtypeStruct((m, attn_width), BF16),
        grid=(batch, n_kv),
        in_specs=[smem, smem,
                  pl.BlockSpec((seq, gw), lambda b, h: (b, h)),
                  pl.BlockSpec((seq, HEAD_DIM), lambda b, h: (b, k_col0 + h)),
                  pl.BlockSpec((seq, HEAD_DIM), lambda b, h: (b, v_col0 + h))],
        out_specs=pl.BlockSpec((seq, gw), lambda b, h: (b, h)),
        scratch_shapes=[pltpu.VMEM((n_kv, 5, rows, BLOCK), F32),
                        pltpu.VMEM((seq, 2 * HEAD_DIM), BF16),
                        pltpu.VMEM((seq // BLOCK, HEAD_DIM, BLOCK), BF16),
                        *[pltpu.VMEM((rows, 3 * BLOCK), F32) for _ in range(2)],
                        *[pltpu.VMEM((rows, 3 * BLOCK), BF16) for _ in range(2)],
                        *[pltpu.VMEM((rows, HEAD_DIM), F32) for _ in range(2)]],
        name="attention",
        compiler_params=_params("arbitrary", "arbitrary"),
    )(slopes, sink, proj, proj, proj)


def _fourier_kernel(u_ref, cd_ref, sd_ref, ch_ref, sh_ref, pm_ref, o_ref,
                    ue_ref, uo_ref, ae_ref, bo_ref, y_ref, *, seq, gd):
    half = seq // 2
    n_blocks = half // BLOCK
    pm = pm_ref[...]

    def reversed_block(src_ref, b, end):
        lo = end - (b + 1) * BLOCK
        if b == 0:
            return jnp.dot(pm[:, :BLOCK], src_ref[lo:end, :], preferred_element_type=F32)
        return jnp.dot(pm, src_ref[lo:lo + 2 * BLOCK, :], preferred_element_type=F32)

    for b in range(n_blocks):
        rows = slice(b * BLOCK, (b + 1) * BLOCK)
        x = u_ref[rows, :].astype(F32)
        r = reversed_block(u_ref, b, seq)
        ue_ref[rows, :] = (x + r).astype(BF16)
        uo_ref[rows, :] = (x - r).astype(BF16)

    mid = []
    for g in range(N_FOURIER_GROUPS):
        cols = slice(g * gd, (g + 1) * gd)
        ae_ref[:, cols] = jnp.dot(ue_ref[:, cols], cd_ref[...],
                                  preferred_element_type=F32).astype(BF16)
        bo_ref[:, cols] = jnp.dot(uo_ref[:, cols], sd_ref[...],
                                  preferred_element_type=F32).astype(BF16)
        mid.append(jnp.dot(u_ref[half:half + BF16_SUBLANE_TILE, cols], cd_ref[...],
                           preferred_element_type=F32)[0:1])
    a_mid = jnp.concatenate(mid, axis=1) * (seq ** -0.5)

    p = jnp.dot(ch_ref[...], ae_ref[...], preferred_element_type=F32)
    q = jnp.dot(sh_ref[...], bo_ref[...], preferred_element_type=F32)
    k_idx = lax.broadcasted_iota(jnp.int32, (half, 1), 0)
    sign = 1.0 - 2.0 * (k_idx & 1).astype(F32)
    base = p[:half] + sign * a_mid
    o_ref[0:half, :] = (base - q).astype(BF16)
    y_ref[...] = (base + q).astype(BF16)
    z_mid = p[half:half + 1] + a_mid

    for b in range(n_blocks):
        r = reversed_block(y_ref, b, half)
        if b == 0:
            first_row = lax.broadcasted_iota(jnp.int32, (BLOCK, 1), 0) == 0
            r = jnp.where(first_row, z_mid, r)
        o_ref[half + b * BLOCK:half + (b + 1) * BLOCK, :] = r.astype(BF16)


def _dft_tables(seq, gd):
    half = seq // 2

    def angles(rows, cols, n):
        k = jnp.arange(rows, dtype=jnp.int32)[:, None]
        s = jnp.arange(cols, dtype=jnp.int32)[None, :]
        return ((k * s) % n).astype(F32) * (2.0 * jnp.pi / n)

    ang_c = angles(gd, gd, gd)
    cd = jnp.cos(ang_c) * (gd ** -0.5)
    sd = jnp.sin(ang_c) * (gd ** -0.5)
    pad_rows = half + BF16_SUBLANE_TILE
    fine = 32
    k = jnp.arange(pad_rows, dtype=jnp.int32)[:, None]
    s_hi = fine * jnp.arange(half // fine, dtype=jnp.int32)[None, :]
    ang_hi = (((k * s_hi) % seq).astype(F32) * (2.0 * jnp.pi / seq))[:, :, None]
    ang_lo = angles(pad_rows, fine, seq)[:, None, :]
    cos_p = (jnp.cos(ang_hi) * jnp.cos(ang_lo) - jnp.sin(ang_hi) * jnp.sin(ang_lo))
    sin_p = (jnp.sin(ang_hi) * jnp.cos(ang_lo) + jnp.cos(ang_hi) * jnp.sin(ang_lo))
    live = jnp.arange(pad_rows)[:, None] <= half
    ch = jnp.where(live, cos_p.reshape(pad_rows, half), 0.0) * (seq ** -0.5)
    sh = sin_p.reshape(pad_rows, half)[:half] * (seq ** -0.5)
    a = jnp.arange(BLOCK)[:, None]
    j = jnp.arange(2 * BLOCK)[None, :]
    pm = jnp.where(a == 0, j == BLOCK, j == BLOCK - a)
    return tuple(t.astype(BF16) for t in (cd, sd, ch, sh, pm))


def _fourier(proj, tables, batch, seq, *, u_col0, f_width):
    m = proj.shape[0]
    gd = f_width // N_FOURIER_GROUPS
    half = seq // 2
    cd, sd, ch, sh, pm = tables
    const = lambda shape: _resident(shape, lambda b: (0, 0))
    return pl.pallas_call(
        functools.partial(_fourier_kernel, seq=seq, gd=gd),
        out_shape=jax.ShapeDtypeStruct((m, f_width), BF16),
        grid=(batch,),
        in_specs=[pl.BlockSpec((seq, f_width), lambda b: (b, u_col0 // f_width)),
                  const(cd.shape), const(sd.shape), const(ch.shape), const(sh.shape),
                  const(pm.shape)],
        out_specs=pl.BlockSpec((seq, f_width), lambda b: (b, 0)),
        scratch_shapes=[pltpu.VMEM((half, f_width), BF16) for _ in range(5)],
        name="fourier",
        compiler_params=_params("parallel"),
    )(proj, cd, sd, ch, sh, pm)


def _merge_kernel(a_ref, f_ref, ga_ref, gf_ref, wa_ref, wf_ref, o_ref):
    ta = jnp.dot(a_ref[...], wa_ref[...], preferred_element_type=F32)
    tf = jnp.dot(f_ref[...], wf_ref[...], preferred_element_type=F32)
    o_ref[...] = (ga_ref[...].astype(F32) * ta + gf_ref[...].astype(F32) * tf).astype(BF16)


def _merge(attn, four, projg, wa, wf, *, gate_col0, tm=1024, tn=1024):
    m, aw = attn.shape
    fw = four.shape[1]
    d = wa.shape[1]
    ga0 = gate_col0 // tn
    gf0 = (gate_col0 + d) // tn
    return pl.pallas_call(
        _merge_kernel,
        out_shape=jax.ShapeDtypeStruct((m, d), BF16),
        grid=(m // tm, d // tn),
        in_specs=[pl.BlockSpec((tm, aw), lambda i, j: (i, 0)),
                  pl.BlockSpec((tm, fw), lambda i, j: (i, 0)),
                  pl.BlockSpec((tm, tn), lambda i, j: (i, ga0 + j)),
                  pl.BlockSpec((tm, tn), lambda i, j: (i, gf0 + j)),
                  pl.BlockSpec((aw, tn), lambda i, j: (0, j)),
                  pl.BlockSpec((fw, tn), lambda i, j: (0, j))],
        out_specs=pl.BlockSpec((tm, tn), lambda i, j: (i, j)),
        name="merge",
        compiler_params=_params("parallel", "arbitrary"),
    )(attn, four, projg, projg, wa, wf)


def _out_kernel(m_ref, x_ref, mod_ref, g_ref, wo_ref, x1_ref, h2_ref):
    out = jnp.dot(m_ref[...], wo_ref[...], preferred_element_type=F32)
    x1 = x_ref[...] + mod_ref[0, 2:3, :] * out
    x1_ref[...] = x1
    h2 = _rms_modulate(x1, g_ref[...], mod_ref[0, 4:5, :], mod_ref[0, 3:4, :])
    h2_ref[...] = h2.astype(BF16)


def _out_proj(merged, x2d, mod, b_off, seq, g_ffn, wo, *, tm=512):
    m, d = x2d.shape
    tiles_per_seq = seq // tm
    return pl.pallas_call(
        _out_kernel,
        out_shape=(jax.ShapeDtypeStruct((m, d), F32), jax.ShapeDtypeStruct((m, d), BF16)),
        grid=(m // tm,),
        in_specs=[pl.BlockSpec((tm, d), lambda i: (i, 0)),
                  pl.BlockSpec((tm, d), lambda i: (i, 0)),
                  pl.BlockSpec((1, N_MOD, d), lambda i: (b_off + i // tiles_per_seq, 0, 0)),
                  pl.BlockSpec((1, d), lambda i: (0, 0)),
                  _resident((d, d), lambda i: (0, 0))],
        out_specs=(pl.BlockSpec((tm, d), lambda i: (i, 0)),
                   pl.BlockSpec((tm, d), lambda i: (i, 0))),
        name="out_proj",
        compiler_params=_params("parallel"),
    )(merged, x2d, mod, g_ffn.reshape(1, d), wo)


def _ffn_up_kernel(h_ref, wg_ref, wu_ref, o_ref):
    h = h_ref[...]
    gt = jnp.dot(h, wg_ref[...], preferred_element_type=F32)
    up = jnp.dot(h, wu_ref[...], preferred_element_type=F32)
    o_ref[...] = (gt * jax.nn.sigmoid(gt) * up).astype(BF16)


def _ffn_up(h2, w_up, *, tm=512, tn=2816):
    m, d = h2.shape
    dff = w_up.shape[1] // 2
    nj = dff // tn
    return pl.pallas_call(
        _ffn_up_kernel,
        out_shape=jax.ShapeDtypeStruct((m, dff), BF16),
        grid=(nj, m // tm),
        in_specs=[pl.BlockSpec((tm, d), lambda j, i: (i, 0)),
                  _resident((d, tn), lambda j, i: (0, j)),
                  _resident((d, tn), lambda j, i: (0, nj + j))],
        out_specs=pl.BlockSpec((tm, tn), lambda j, i: (i, j)),
        name="ffn_up",
        compiler_params=_params("arbitrary", "parallel"),
    )(h2, w_up, w_up)


def _ffn_down_kernel(a_ref, x1_ref, mod_ref, g_ref, wd_ref, y_ref, *, final_norm):
    out = jnp.dot(a_ref[...], wd_ref[...], preferred_element_type=F32)
    x2 = x1_ref[...] + mod_ref[0, 5:6, :] * out
    if final_norm:
        y = x2 * lax.rsqrt(jnp.mean(x2 * x2, axis=-1, keepdims=True) + RMS_EPS)
        x2 = y * g_ref[...]
    y_ref[...] = x2


def _ffn_down(act, x1, mod, b_off, seq, g_final, wd, *, final_norm, tm=512):
    m, d = x1.shape
    dff = act.shape[1]
    tiles_per_seq = seq // tm
    return pl.pallas_call(
        functools.partial(_ffn_down_kernel, final_norm=final_norm),
        out_shape=jax.ShapeDtypeStruct((m, d), F32),
        grid=(m // tm,),
        in_specs=[pl.BlockSpec((tm, dff), lambda i: (i, 0)),
                  pl.BlockSpec((tm, d), lambda i: (i, 0)),
                  pl.BlockSpec((1, N_MOD, d), lambda i: (b_off + i // tiles_per_seq, 0, 0)),
                  pl.BlockSpec((1, d), lambda i: (0, 0)),
                  _resident((dff, d), lambda i: (0, 0))],
        out_specs=pl.BlockSpec((tm, d), lambda i: (i, 0)),
        name="ffn_down",
        compiler_params=_params("parallel"),
    )(act, x1, mod, g_final.reshape(1, d), wd)


def kernel(x_prompt, x_sample, c_prompt, c_sample, w_mod, b_mod, g_mix, w_in, attn_sink,
           w_attn_branch, w_fourier_branch, w_gate, b_gate, w_out, g_ffn, w_up, w_down,
           g_final):
    depth, d, in_width = w_in.shape
    attn_width = w_attn_branch.shape[1]
    f_width = w_fourier_branch.shape[1]
    kv_width = (in_width - attn_width - f_width) // 2
    n_heads = attn_width // HEAD_DIM
    groups = [(x_prompt, 0), (x_sample, c_prompt.shape[0])]
    seq = x_prompt.shape[1]
    assert x_sample.shape[1] == seq and seq % (4 * BLOCK) == 0
    assert (attn_width + 2 * kv_width) % f_width == 0
    assert w_gate.shape[2] == in_width

    c_all = jnp.concatenate([c_prompt, c_sample], axis=0)
    pad = -c_all.shape[0] % BF16_SUBLANE_TILE
    c_all = jnp.pad(c_all, ((0, pad), (0, 0)))

    heads = jnp.arange(1, n_heads + 1, dtype=F32)
    slopes = jnp.exp2(-8.0 * heads / n_heads)
    dft_tables = _dft_tables(seq, f_width // N_FOURIER_GROUPS)

    xs = [x.reshape(-1, d) for x, _ in groups]
    for l in range(depth):
        last = l == depth - 1
        mod = _modulation(c_all, w_mod[l], b_mod[l]).reshape(-1, N_MOD, d)
        w_pair = jnp.stack([w_in[l], w_gate[l]]).astype(BF16)
        wa = w_attn_branch[l].astype(BF16)
        wf = w_fourier_branch[l].astype(BF16)
        wo = w_out[l].astype(BF16)
        wu = w_up[l].astype(BF16)
        wd = w_down[l].astype(BF16)
        sink = attn_sink[l].astype(F32)
        new_xs = []
        for x2d, (x_in, b_off) in zip(xs, groups):
            batch = x_in.shape[0]
            projg = _projection(x2d, mod, b_off, seq, g_mix[l], w_pair, b_gate[l])
            attn = _attention(projg, slopes, sink, batch, seq,
                              attn_width=attn_width, kv_width=kv_width)
            four = _fourier(projg, dft_tables, batch, seq,
                            u_col0=attn_width + 2 * kv_width, f_width=f_width)
            merged = _merge(attn, four, projg, wa, wf, gate_col0=in_width)
            x1, h2 = _out_proj(merged, x2d, mod, b_off, seq, g_ffn[l], wo)
            act = _ffn_up(h2, wu)
            new_xs.append(_ffn_down(act, x1, mod, b_off, seq, g_final, wd, final_norm=last))
        xs = new_xs
    return tuple(x2d.reshape(x_in.shape) for x2d, (x_in, _) in zip(xs, groups))
```

```python
import functools
import math

import jax
import jax.numpy as jnp
from jax import lax
from jax.experimental import pallas as pl
from jax.experimental.pallas import tpu as pltpu

F32 = jnp.float32
BF16 = jnp.bfloat16

HEAD_DIM = 128
Q_PER_KV = 4
N_FOURIER_GROUPS = 4
WINDOW = 128
BLOCK = 128
N_MOD = 6
RMS_EPS = 1e-6
LOG2E = math.log2(math.e)

V7X_VMEM_LIMIT_BYTES = 60 * 1024 * 1024
BF16_SUBLANE_TILE = 16


def _params(*semantics):
    return pltpu.CompilerParams(dimension_semantics=semantics,
                                vmem_limit_bytes=V7X_VMEM_LIMIT_BYTES)


def _resident(block_shape, index_map):
    return pl.BlockSpec(block_shape, index_map, pipeline_mode=pl.Buffered(1))


def _rms_modulate(x, gain, scale, shift):
    y = x * lax.rsqrt(jnp.mean(x * x, axis=-1, keepdims=True) + RMS_EPS)
    return y * (gain * (1.0 + scale)) + shift


def _mod_kernel(c_ref, w_ref, b_ref, o_ref):
    c = c_ref[...]
    a = (c * jax.nn.sigmoid(c)).astype(BF16)
    o_ref[...] = jnp.dot(a, w_ref[...].astype(BF16), preferred_element_type=F32) + b_ref[...]


def _modulation(c, w_mod, b_mod, *, tn=512):
    bp, d = c.shape
    n = w_mod.shape[1]
    return pl.pallas_call(
        _mod_kernel,
        out_shape=jax.ShapeDtypeStruct((bp, n), F32),
        grid=(n // tn,),
        in_specs=[pl.BlockSpec((bp, d), lambda j: (0, 0)),
                  pl.BlockSpec((d, tn), lambda j: (0, j)),
                  pl.BlockSpec((1, tn), lambda j: (0, j))],
        out_specs=pl.BlockSpec((bp, tn), lambda j: (0, j)),
        name="modulation",
        compiler_params=_params("arbitrary"),
    )(c, w_mod, b_mod.reshape(1, n))


def _proj_kernel(x0_ref, xn_ref, mod0_ref, modn_ref, g_ref, w_ref, b_ref, o_ref,
                 h_even_ref, h_odd_ref, *, n_in_tiles, n_col_tiles, row_chunk):
    i = pl.program_id(0)
    j = pl.program_id(1)
    odd = lax.rem(i, 2) == 1
    gain = g_ref[...]
    tm = xn_ref.shape[0]
    rows_per_step = tm // n_col_tiles

    def norm_rows(x_ref, mod_ref, dst_ref, row0, n_rows):
        rows = pl.ds(pl.multiple_of(row0, n_rows), n_rows)
        h = _rms_modulate(x_ref[rows, :], gain, mod_ref[0, 1:2, :], mod_ref[0, 0:1, :])
        dst_ref[rows, :] = h.astype(BF16)
        return h

    @pl.when((i == 0) & (j == 0))
    def _():
        def body(r, carry):
            norm_rows(x0_ref, mod0_ref, h_even_ref, r * row_chunk, row_chunk)
            return carry

        lax.fori_loop(0, tm // row_chunk, body, 0)

    def column_step(cur_ref, nxt_ref, gated):
        h_next = norm_rows(xn_ref, modn_ref, nxt_ref, j * rows_per_step, rows_per_step)
        zero = jnp.where(pl.program_id(0) < 0, jnp.max(h_next), 0.0)
        tn = o_ref.shape[1]
        for lo, anchor in ((0, 0.0), (tn // 2, zero)):
            cols = slice(lo, lo + tn // 2)
            acc = jnp.dot(cur_ref[...], w_ref[:, cols], preferred_element_type=F32) + anchor
            if gated:
                half_z = (0.5 * (acc + b_ref[:, cols])).astype(BF16)
                o_ref[:, cols] = 0.5 * jnp.tanh(half_z) + 0.5
            else:
                o_ref[:, cols] = acc.astype(BF16)

    for is_odd, cur_ref, nxt_ref in ((False, h_even_ref, h_odd_ref),
                                     (True, h_odd_ref, h_even_ref)):
        for gated in (False, True):
            pl.when((odd == is_odd) & ((j >= n_in_tiles) == gated))(
                functools.partial(column_step, cur_ref, nxt_ref, gated))


def _projection(x2d, mod, b_off, seq, g_mix, w_pair, b_gate, *, tm=1024, tn=1024):
    m, d = x2d.shape
    n = 2 * w_pair.shape[2]
    n_in_tiles = w_pair.shape[2] // tn
    n_col_tiles = n // tn
    tiles_per_seq = seq // tm
    last = m // tm - 1
    kern = functools.partial(_proj_kernel, n_in_tiles=n_in_tiles, n_col_tiles=n_col_tiles,
                             row_chunk=256)
    nxt = lambda i: jnp.minimum(i + 1, last)
    return pl.pallas_call(
        kern,
        out_shape=jax.ShapeDtypeStruct((m, n), BF16),
        grid=(m // tm, n_col_tiles),
        in_specs=[_resident((tm, d), lambda i, j: (0, 0)),
                  pl.BlockSpec((tm, d), lambda i, j: (nxt(i), 0)),
                  _resident((1, N_MOD, d), lambda i, j: (b_off, 0, 0)),
                  pl.BlockSpec((1, N_MOD, d),
                               lambda i, j: (b_off + nxt(i) // tiles_per_seq, 0, 0)),
                  pl.BlockSpec((1, d), lambda i, j: (0, 0)),
                  pl.BlockSpec((None, d, tn),
                               lambda i, j: (j // n_in_tiles, 0, lax.rem(j, n_in_tiles))),
                  pl.BlockSpec((1, tn), lambda i, j: (0, jnp.maximum(j - n_in_tiles, 0)))],
        out_specs=pl.BlockSpec((tm, tn), lambda i, j: (i, j)),
        scratch_shapes=[pltpu.VMEM((tm, d), BF16), pltpu.VMEM((tm, d), BF16)],
        name="projection",
        compiler_params=_params("arbitrary", "arbitrary"),
    )(x2d, x2d, mod, mod, g_mix.reshape(1, d), w_pair, b_gate.reshape(1, -1))


def _attn_kernel(slopes_ref, sink_ref, q_ref, k_ref, v_ref, o_ref,
                 bias_ref, v1_ref, kt_ref, s0_ref, s1_ref, p0_ref, p1_ref, e0_ref, e1_ref, *,
                 seq):
    s_refs, p_refs, e_refs = (s0_ref, s1_ref), (p0_ref, p1_ref), (e0_ref, e1_ref)
    kh = pl.program_id(1)
    n_chunks = 3
    span = n_chunks * BLOCK
    n_blocks = seq // BLOCK
    logit_scale = (HEAD_DIM ** -0.5) * LOG2E

    @pl.when(pl.program_id(0) == 0)
    def _():
        q_pos = lax.broadcasted_iota(jnp.int32, (BLOCK, BLOCK), 0)
        k_pos = lax.broadcasted_iota(jnp.int32, (BLOCK, BLOCK), 1)
        for d in range(-2, 3):
            rel = d * BLOCK + k_pos - q_pos
            valid = jnp.abs(rel) <= WINDOW
            dist = jnp.abs(rel).astype(F32)
            for g in range(Q_PER_KV):
                slope2 = slopes_ref[kh * Q_PER_KV + g] * LOG2E
                bias_ref[kh, d + 2, g * BLOCK:(g + 1) * BLOCK, :] = jnp.where(
                    valid, -slope2 * dist, -jnp.inf)
        v1_ref[:, HEAD_DIM:] = jnp.ones((seq, HEAD_DIM), BF16)

    sinks2 = [sink_ref[kh * Q_PER_KV + g] * LOG2E for g in range(Q_PER_KV)]
    v1_ref[:, :HEAD_DIM] = v_ref[...]
    for c in range(n_blocks):
        kt_ref[c] = k_ref[c * BLOCK:(c + 1) * BLOCK, :].T

    def window(i):
        q0 = pl.multiple_of(i * BLOCK, BLOCK)
        k0 = pl.multiple_of(jnp.clip(q0 - BLOCK, 0, seq - span), BLOCK)
        return q0, k0

    def logits(i, s_ref):
        q0, k0 = window(i)
        first = lax.div(k0 - q0 + 2 * BLOCK, BLOCK)
        q = q_ref[pl.ds(q0, BLOCK), :]
        qs = jnp.concatenate([q[:, g * HEAD_DIM:(g + 1) * HEAD_DIM] for g in range(Q_PER_KV)],
                             axis=0)
        c0 = lax.div(k0, BLOCK)
        kw_t = jnp.concatenate([kt_ref[c0 + t] for t in range(n_chunks)], axis=1)
        s = jnp.dot(qs, kw_t, preferred_element_type=F32)
        for t in range(n_chunks):
            cols = slice(t * BLOCK, (t + 1) * BLOCK)
            s_ref[:, cols] = s[:, cols] * logit_scale + bias_ref[kh, first + t]

    def softmax(s_ref, p_ref, e_ref):
        for g in range(Q_PER_KV):
            rows = slice(g * BLOCK, (g + 1) * BLOCK)
            a = s_ref[rows, :]
            mx = jnp.maximum(jnp.max(a, axis=-1, keepdims=True), sinks2[g])
            p_ref[rows, :] = jnp.exp2(a - mx).astype(BF16)
            e_ref[rows, :] = jnp.broadcast_to(jnp.exp2(sinks2[g] - mx), (BLOCK, HEAD_DIM))

    def values(i, p_ref, e_ref):
        q0, k0 = window(i)
        r = jnp.dot(p_ref[...], v1_ref[pl.ds(k0, span), :], preferred_element_type=F32)
        out = (r[:, :HEAD_DIM] / (r[:, HEAD_DIM:] + e_ref[...])).astype(BF16)
        for g in range(Q_PER_KV):
            o_ref[pl.ds(q0, BLOCK), g * HEAD_DIM:(g + 1) * HEAD_DIM] = (
                out[g * BLOCK:(g + 1) * BLOCK])

    def pair(i, n_logits, n_softmax):
        for half in range(2):
            blk = i + half
            if half < n_logits:
                logits(blk + 2, s_refs[half])
            if half < n_softmax:
                softmax(s_refs[1 - half], p_refs[1 - half], e_refs[1 - half])
            values(blk, p_refs[half], e_refs[half])

    logits(0, s_refs[0])
    logits(1, s_refs[1])
    softmax(s_refs[0], p_refs[0], e_refs[0])

    def body(it, carry):
        pair(4 * it, 2, 2)
        pair(4 * it + 2, 2, 2)
        return carry

    full_pairs = n_blocks // 2 - 1
    lax.fori_loop(0, full_pairs // 2, body, 0)
    if full_pairs % 2:
        pair(n_blocks - 4, 2, 2)
    pair(n_blocks - 2, 0, 1)


def _attention(proj, slopes, sink, batch, seq, *, attn_width, kv_width):
    m = proj.shape[0]
    n_kv = kv_width // HEAD_DIM
    gw = Q_PER_KV * HEAD_DIM
    k_col0 = attn_width // HEAD_DIM
    v_col0 = (attn_width + kv_width) // HEAD_DIM
    smem = pl.BlockSpec(memory_space=pltpu.SMEM)
    rows = Q_PER_KV * BLOCK
    return pl.pallas_call(
        functools.partial(_attn_kernel, seq=seq),
        out_shape=jax.ShapeDtypeStruct((m, attn_width), BF16),
        grid=(batch, n_kv),
        in_specs=[smem, smem,
                  pl.BlockSpec((seq, gw), lambda b, h: (b, h)),
                  pl.BlockSpec((seq, HEAD_DIM), lambda b, h: (b, k_col0 + h)),
                  pl.BlockSpec((seq, HEAD_DIM), lambda b, h: (b, v_col0 + h))],
        out_specs=pl.BlockSpec((seq, gw), lambda b, h: (b, h)),
        scratch_shapes=[pltpu.VMEM((n_kv, 5, rows, BLOCK), F32),
                        pltpu.VMEM((seq, 2 * HEAD_DIM), BF16),
                        pltpu.VMEM((seq // BLOCK, HEAD_DIM, BLOCK), BF16),
                        *[pltpu.VMEM((rows, 3 * BLOCK), F32) for _ in range(2)],
                        *[pltpu.VMEM((rows, 3 * BLOCK), BF16) for _ in range(2)],
                        *[pltpu.VMEM((rows, HEAD_DIM), F32) for _ in range(2)]],
        name="attention",
        compiler_params=_params("arbitrary", "arbitrary"),
    )(slopes, sink, proj, proj, proj)


def _fourier_kernel(u_ref, cd_ref, sd_ref, ch_ref, sh_ref, pm_ref, o_ref,
                    ue_ref, uo_ref, ae_ref, bo_ref, y_ref, *, seq, gd):
    half = seq // 2
    n_blocks = half // BLOCK
    pm = pm_ref[...]

    def reversed_block(src_ref, b, end):
        lo = end - (b + 1) * BLOCK
        if b == 0:
            return jnp.dot(pm[:, :BLOCK], src_ref[lo:end, :], preferred_element_type=F32)
        return jnp.dot(pm, src_ref[lo:lo + 2 * BLOCK, :], preferred_element_type=F32)

    for b in range(n_blocks):
        rows = slice(b * BLOCK, (b + 1) * BLOCK)
        x = u_ref[rows, :].astype(F32)
        r = reversed_block(u_ref, b, seq)
        ue_ref[rows, :] = (x + r).astype(BF16)
        uo_ref[rows, :] = (x - r).astype(BF16)

    mid = []
    for g in range(N_FOURIER_GROUPS):
        cols = slice(g * gd, (g + 1) * gd)
        ae_ref[:, cols] = jnp.dot(ue_ref[:, cols], cd_ref[...],
                                  preferred_element_type=F32).astype(BF16)
        bo_ref[:, cols] = jnp.dot(uo_ref[:, cols], sd_ref[...],
                                  preferred_element_type=F32).astype(BF16)
        mid.append(jnp.dot(u_ref[half:half + BF16_SUBLANE_TILE, cols], cd_ref[...],
                           preferred_element_type=F32)[0:1])
    a_mid = jnp.concatenate(mid, axis=1) * (seq ** -0.5)

    p = jnp.dot(ch_ref[...], ae_ref[...], preferred_element_type=F32)
    q = jnp.dot(sh_ref[...], bo_ref[...], preferred_element_type=F32)
    k_idx = lax.broadcasted_iota(jnp.int32, (half, 1), 0)
    sign = 1.0 - 2.0 * (k_idx & 1).astype(F32)
    base = p[:half] + sign * a_mid
    o_ref[0:half, :] = (base - q).astype(BF16)
    y_ref[...] = (base + q).astype(BF16)
    z_mid = p[half:half + 1] + a_mid

    for b in range(n_blocks):
        r = reversed_block(y_ref, b, half)
        if b == 0:
            first_row = lax.broadcasted_iota(jnp.int32, (BLOCK, 1), 0) == 0
            r = jnp.where(first_row, z_mid, r)
        o_ref[half + b * BLOCK:half + (b + 1) * BLOCK, :] = r.astype(BF16)


def _dft_tables(seq, gd):
    half = seq // 2

    def angles(rows, cols, n):
        k = jnp.arange(rows, dtype=jnp.int32)[:, None]
        s = jnp.arange(cols, dtype=jnp.int32)[None, :]
        return ((k * s) % n).astype(F32) * (2.0 * jnp.pi / n)

    ang_c = angles(gd, gd, gd)
    cd = jnp.cos(ang_c) * (gd ** -0.5)
    sd = jnp.sin(ang_c) * (gd ** -0.5)
    pad_rows = half + BF16_SUBLANE_TILE
    fine = 32
    k = jnp.arange(pad_rows, dtype=jnp.int32)[:, None]
    s_hi = fine * jnp.arange(half // fine, dtype=jnp.int32)[None, :]
    ang_hi = (((k * s_hi) % seq).astype(F32) * (2.0 * jnp.pi / seq))[:, :, None]
    ang_lo = angles(pad_rows, fine, seq)[:, None, :]
    cos_p = (jnp.cos(ang_hi) * jnp.cos(ang_lo) - jnp.sin(ang_hi) * jnp.sin(ang_lo))
    sin_p = (jnp.sin(ang_hi) * jnp.cos(ang_lo) + jnp.cos(ang_hi) * jnp.sin(ang_lo))
    live = jnp.arange(pad_rows)[:, None] <= half
    ch = jnp.where(live, cos_p.reshape(pad_rows, half), 0.0) * (seq ** -0.5)
    sh = sin_p.reshape(pad_rows, half)[:half] * (seq ** -0.5)
    a = jnp.arange(BLOCK)[:, None]
    j = jnp.arange(2 * BLOCK)[None, :]
    pm = jnp.where(a == 0, j == BLOCK, j == BLOCK - a)
    return tuple(t.astype(BF16) for t in (cd, sd, ch, sh, pm))


def _fourier(proj, tables, batch, seq, *, u_col0, f_width):
    m = proj.shape[0]
    gd = f_width // N_FOURIER_GROUPS
    half = seq // 2
    cd, sd, ch, sh, pm = tables
    const = lambda shape: _resident(shape, lambda b: (0, 0))
    return pl.pallas_call(
        functools.partial(_fourier_kernel, seq=seq, gd=gd),
        out_shape=jax.ShapeDtypeStruct((m, f_width), BF16),
        grid=(batch,),
        in_specs=[pl.BlockSpec((seq, f_width), lambda b: (b, u_col0 // f_width)),
                  const(cd.shape), const(sd.shape), const(ch.shape), const(sh.shape),
                  const(pm.shape)],
        out_specs=pl.BlockSpec((seq, f_width), lambda b: (b, 0)),
        scratch_shapes=[pltpu.VMEM((half, f_width), BF16) for _ in range(5)],
        name="fourier",
        compiler_params=_params("parallel"),
    )(proj, cd, sd, ch, sh, pm)


def _merge_kernel(a_ref, f_ref, ga_ref, gf_ref, wa_ref, wf_ref, o_ref):
    ta = jnp.dot(a_ref[...], wa_ref[...], preferred_element_type=F32)
    tf = jnp.dot(f_ref[...], wf_ref[...], preferred_element_type=F32)
    o_ref[...] = (ga_ref[...].astype(F32) * ta + gf_ref[...].astype(F32) * tf).astype(BF16)


def _merge(attn, four, projg, wa, wf, *, gate_col0, tm=1024, tn=1024):
    m, aw = attn.shape
    fw = four.shape[1]
    d = wa.shape[1]
    ga0 = gate_col0 // tn
    gf0 = (gate_col0 + d) // tn
    return pl.pallas_call(
        _merge_kernel,
        out_shape=jax.ShapeDtypeStruct((m, d), BF16),
        grid=(m // tm, d // tn),
        in_specs=[pl.BlockSpec((tm, aw), lambda i, j: (i, 0)),
                  pl.BlockSpec((tm, fw), lambda i, j: (i, 0)),
                  pl.BlockSpec((tm, tn), lambda i, j: (i, ga0 + j)),
                  pl.BlockSpec((tm, tn), lambda i, j: (i, gf0 + j)),
                  pl.BlockSpec((aw, tn), lambda i, j: (0, j)),
                  pl.BlockSpec((fw, tn), lambda i, j: (0, j))],
        out_specs=pl.BlockSpec((tm, tn), lambda i, j: (i, j)),
        name="merge",
        compiler_params=_params("parallel", "arbitrary"),
    )(attn, four, projg, projg, wa, wf)


def _out_kernel(m_ref, x_ref, mod_ref, g_ref, wo_ref, x1_ref, h2_ref):
    out = jnp.dot(m_ref[...], wo_ref[...], preferred_element_type=F32)
    x1 = x_ref[...] + mod_ref[0, 2:3, :] * out
    x1_ref[...] = x1
    h2 = _rms_modulate(x1, g_ref[...], mod_ref[0, 4:5, :], mod_ref[0, 3:4, :])
    h2_ref[...] = h2.astype(BF16)


def _out_proj(merged, x2d, mod, b_off, seq, g_ffn, wo, *, tm=512):
    m, d = x2d.shape
    tiles_per_seq = seq // tm
    return pl.pallas_call(
        _out_kernel,
        out_shape=(jax.ShapeDtypeStruct((m, d), F32), jax.ShapeDtypeStruct((m, d), BF16)),
        grid=(m // tm,),
        in_specs=[pl.BlockSpec((tm, d), lambda i: (i, 0)),
                  pl.BlockSpec((tm, d), lambda i: (i, 0)),
                  pl.BlockSpec((1, N_MOD, d), lambda i: (b_off + i // tiles_per_seq, 0, 0)),
                  pl.BlockSpec((1, d), lambda i: (0, 0)),
                  _resident((d, d), lambda i: (0, 0))],
        out_specs=(pl.BlockSpec((tm, d), lambda i: (i, 0)),
                   pl.BlockSpec((tm, d), lambda i: (i, 0))),
        name="out_proj",
        compiler_params=_params("parallel"),
    )(merged, x2d, mod, g_ffn.reshape(1, d), wo)


def _merge_out_kernel(a_ref, f_ref, ga_ref, gf_ref, x_ref, mod_ref, g_ref, wa_ref, wf_ref,
                      wo_ref, x1_ref, h2_ref):
    ta = jnp.dot(a_ref[...], wa_ref[...], preferred_element_type=F32)
    tf = jnp.dot(f_ref[...], wf_ref[...], preferred_element_type=F32)
    merged = (ga_ref[...].astype(F32) * ta + gf_ref[...].astype(F32) * tf).astype(BF16)
    out = jnp.dot(merged, wo_ref[...], preferred_element_type=F32)
    x1 = x_ref[...] + mod_ref[0, 2:3, :] * out
    x1_ref[...] = x1
    h2 = _rms_modulate(x1, g_ref[...], mod_ref[0, 4:5, :], mod_ref[0, 3:4, :])
    h2_ref[...] = h2.astype(BF16)


def _merge_out(attn, four, projg, x2d, mod, b_off, seq, g_ffn, wa, wf, wo, *, gate_col0,
               tm=256):
    m, aw = attn.shape
    fw = four.shape[1]
    d = wa.shape[1]
    tiles_per_seq = seq // tm
    ga0 = gate_col0 // d
    row = lambda shape: pl.BlockSpec(shape, lambda i: (i, 0))
    return pl.pallas_call(
        _merge_out_kernel,
        out_shape=(jax.ShapeDtypeStruct((m, d), F32), jax.ShapeDtypeStruct((m, d), BF16)),
        grid=(m // tm,),
        in_specs=[row((tm, aw)), row((tm, fw)),
                  pl.BlockSpec((tm, d), lambda i: (i, ga0)),
                  pl.BlockSpec((tm, d), lambda i: (i, ga0 + 1)),
                  row((tm, d)),
                  pl.BlockSpec((1, N_MOD, d), lambda i: (b_off + i // tiles_per_seq, 0, 0)),
                  pl.BlockSpec((1, d), lambda i: (0, 0)),
                  _resident((aw, d), lambda i: (0, 0)),
                  _resident((fw, d), lambda i: (0, 0)),
                  _resident((d, d), lambda i: (0, 0))],
        out_specs=(row((tm, d)), row((tm, d))),
        name="merge_out",
        compiler_params=_params("parallel"),
    )(attn, four, projg, projg, x2d, mod, g_ffn.reshape(1, d), wa, wf, wo)


def _ffn_up_kernel(h_ref, wg_ref, wu_ref, o_ref):
    h = h_ref[...]
    gt = jnp.dot(h, wg_ref[...], preferred_element_type=F32)
    up = jnp.dot(h, wu_ref[...], preferred_element_type=F32)
    o_ref[...] = (gt * jax.nn.sigmoid(gt) * up).astype(BF16)


def _ffn_up(h2, w_up, *, tm=512, tn=2816):
    m, d = h2.shape
    dff = w_up.shape[1] // 2
    nj = dff // tn
    return pl.pallas_call(
        _ffn_up_kernel,
        out_shape=jax.ShapeDtypeStruct((m, dff), BF16),
        grid=(nj, m // tm),
        in_specs=[pl.BlockSpec((tm, d), lambda j, i: (i, 0)),
                  _resident((d, tn), lambda j, i: (0, j)),
                  _resident((d, tn), lambda j, i: (0, nj + j))],
        out_specs=pl.BlockSpec((tm, tn), lambda j, i: (i, j)),
        name="ffn_up",
        compiler_params=_params("arbitrary", "parallel"),
    )(h2, w_up, w_up)


def _ffn_down_kernel(a_ref, x1_ref, mod_ref, g_ref, wd_ref, y_ref, *, final_norm):
    out = jnp.dot(a_ref[...], wd_ref[...], preferred_element_type=F32)
    x2 = x1_ref[...] + mod_ref[0, 5:6, :] * out
    if final_norm:
        y = x2 * lax.rsqrt(jnp.mean(x2 * x2, axis=-1, keepdims=True) + RMS_EPS)
        x2 = y * g_ref[...]
    y_ref[...] = x2


def _ffn_down(act, x1, mod, b_off, seq, g_final, wd, *, final_norm, tm=512):
    m, d = x1.shape
    dff = act.shape[1]
    tiles_per_seq = seq // tm
    return pl.pallas_call(
        functools.partial(_ffn_down_kernel, final_norm=final_norm),
        out_shape=jax.ShapeDtypeStruct((m, d), F32),
        grid=(m // tm,),
        in_specs=[pl.BlockSpec((tm, dff), lambda i: (i, 0)),
                  pl.BlockSpec((tm, d), lambda i: (i, 0)),
                  pl.BlockSpec((1, N_MOD, d), lambda i: (b_off + i // tiles_per_seq, 0, 0)),
                  pl.BlockSpec((1, d), lambda i: (0, 0)),
                  _resident((dff, d), lambda i: (0, 0))],
        out_specs=pl.BlockSpec((tm, d), lambda i: (i, 0)),
        name="ffn_down",
        compiler_params=_params("parallel"),
    )(act, x1, mod, g_final.reshape(1, d), wd)


def kernel(x_prompt, x_sample, c_prompt, c_sample, w_mod, b_mod, g_mix, w_in, attn_sink,
           w_attn_branch, w_fourier_branch, w_gate, b_gate, w_out, g_ffn, w_up, w_down,
           g_final):
    depth, d, in_width = w_in.shape
    attn_width = w_attn_branch.shape[1]
    f_width = w_fourier_branch.shape[1]
    kv_width = (in_width - attn_width - f_width) // 2
    n_heads = attn_width // HEAD_DIM
    groups = [(x_prompt, 0), (x_sample, c_prompt.shape[0])]
    seq = x_prompt.shape[1]
    assert x_sample.shape[1] == seq and seq % (4 * BLOCK) == 0
    assert (attn_width + 2 * kv_width) % f_width == 0
    assert w_gate.shape[2] == in_width

    c_all = jnp.concatenate([c_prompt, c_sample], axis=0)
    pad = -c_all.shape[0] % BF16_SUBLANE_TILE
    c_all = jnp.pad(c_all, ((0, pad), (0, 0)))

    heads = jnp.arange(1, n_heads + 1, dtype=F32)
    slopes = jnp.exp2(-8.0 * heads / n_heads)
    dft_tables = _dft_tables(seq, f_width // N_FOURIER_GROUPS)

    xs = [x.reshape(-1, d) for x, _ in groups]
    for l in range(depth):
        last = l == depth - 1
        mod = _modulation(c_all, w_mod[l], b_mod[l]).reshape(-1, N_MOD, d)
        w_pair = jnp.stack([w_in[l], w_gate[l]]).astype(BF16)
        wa = w_attn_branch[l].astype(BF16)
        wf = w_fourier_branch[l].astype(BF16)
        wo = w_out[l].astype(BF16)
        wu = w_up[l].astype(BF16)
        wd = w_down[l].astype(BF16)
        sink = attn_sink[l].astype(F32)
        new_xs = []
        for x2d, (x_in, b_off) in zip(xs, groups):
            batch = x_in.shape[0]
            projg = _projection(x2d, mod, b_off, seq, g_mix[l], w_pair, b_gate[l])
            attn = _attention(projg, slopes, sink, batch, seq,
                              attn_width=attn_width, kv_width=kv_width)
            four = _fourier(projg, dft_tables, batch, seq,
                            u_col0=attn_width + 2 * kv_width, f_width=f_width)
            x1, h2 = _merge_out(attn, four, projg, x2d, mod, b_off, seq, g_ffn[l], wa, wf, wo,
                                gate_col0=in_width)
            act = _ffn_up(h2, wu)
            new_xs.append(_ffn_down(act, x1, mod, b_off, seq, g_final, wd, final_norm=last))
        xs = new_xs
    return tuple(x2d.reshape(x_in.shape) for x2d, (x_in, _) in zip(xs, groups))
```
